```python
import jax, jax.numpy as jnp
from jax import lax
import numpy as np

D_MODEL = 2048
BATCH = 4
SEQ = 2048
DEPTH = 2
DEC_BATCH = 4
DEC_SEQ = 8192
PAST_LEN = 128

MIX_W = D_MODEL
POOL_W = MIX_W // 4
N_POOL = 4
POOL_CH = POOL_W // N_POOL
POOL_WINDOWS = (2, 4, 8, 16)
CONV_W = MIX_W // 2
CONV_K = 31
FOURIER_W = MIX_W - POOL_W - CONV_W
N_FOURIER = 4
FOURIER_CH = FOURIER_W // N_FOURIER
IN_COLS = POOL_W + 2 * CONV_W + FOURIER_W
N_GROUPS = 4
EXPERTS_PER_GROUP = 8
N_EXPERTS = N_GROUPS * EXPERTS_PER_GROUP
TOP_K = 2
D_EXPERT = D_MODEL // 4
ROUTE_BLOCK = 128
EPS = 1e-6

kernel_name = "hybrid_pool_conv_fourier_hmoe_encoder"


def rmsnorm(x, g):
    xf = x.astype(jnp.float32)
    y = xf * lax.rsqrt(jnp.mean(xf * xf, axis=-1, keepdims=True) + EPS)
    return (y * g.astype(jnp.float32)).astype(x.dtype)


def layernorm(x, g, b):
    xf = x.astype(jnp.float32)
    xc = xf - jnp.mean(xf, axis=-1, keepdims=True)
    var = jnp.mean(xc * xc, axis=-1, keepdims=True)
    y = xc * lax.rsqrt(var + EPS) * g.astype(jnp.float32) + b.astype(jnp.float32)
    return y.astype(x.dtype)


def pool_mixer(u, w_grp, scale):
    B, S, _ = u.shape
    ug = u.astype(jnp.float32).reshape(B, S, N_POOL, POOL_CH)
    cs = jnp.pad(jnp.cumsum(ug, axis=1), ((0, 0), (1, 0), (0, 0), (0, 0)))
    t = jnp.arange(S)
    diffs = []
    for gi, w in enumerate(POOL_WINDOWS):
        lo = jnp.clip(t - w // 2, 0, S)
        hi = jnp.clip(t + (w - w // 2), 0, S)
        csg = cs[:, :, gi]
        mean = (csg[:, hi] - csg[:, lo]) / (hi - lo).astype(jnp.float32)[None, :, None]
        diffs.append(mean - ug[:, :, gi])
    d = jnp.stack(diffs, axis=2).astype(u.dtype)
    y = jnp.einsum('bsgc,gce->bsge', d, w_grp).reshape(B, S, POOL_W)
    return y * scale


def conv_module(u, dw_w, dw_b, ln_g, ln_b, pw_w, pw_b):
    a, b = jnp.split(u, 2, axis=-1)
    v = a * jax.nn.sigmoid(b)
    v = lax.conv_general_dilated(
        v, dw_w[:, None, :].astype(v.dtype), window_strides=(1,),
        padding=((CONV_K // 2, CONV_K // 2),),
        dimension_numbers=('NWC', 'WIO', 'NWC'),
        feature_group_count=CONV_W) + dw_b
    v = jax.nn.silu(layernorm(v, ln_g, ln_b))
    return v @ pw_w + pw_b


def fourier_mixer(u, w):
    B, S, _ = u.shape
    z = jnp.fft.fft2(u.astype(jnp.float32).reshape(B, S, N_FOURIER, FOURIER_CH), axes=(1, 3), norm='ortho')
    return jnp.real(z).reshape(B, S, FOURIER_W).astype(u.dtype) @ w


def hier_moe(h, wc, bc, wf, bf, w1, w3, w2):
    N, D = h.shape
    A = N * TOP_K
    lc = (h @ wc).astype(jnp.float32) + bc.astype(jnp.float32)
    pc = jax.nn.softmax(lc, axis=-1)
    grp = jnp.argmax(lc, axis=-1)
    p_grp = jnp.take_along_axis(pc, grp[:, None], axis=1)
    lf = jnp.einsum('nd,gde->nge', h, wf).astype(jnp.float32) + bf.astype(jnp.float32)
    lf = jnp.take_along_axis(lf, grp[:, None, None], axis=1)[:, 0]
    top_v, top_i = lax.top_k(lf, TOP_K)
    wts = (p_grp * jax.nn.softmax(top_v, axis=-1)).reshape(A)
    eid = (grp[:, None] * EXPERTS_PER_GROUP + top_i).reshape(A)
    tok = jnp.repeat(jnp.arange(N, dtype=jnp.int32), TOP_K)
    order = jnp.argsort(eid)
    e_s = eid[order]
    counts = jnp.bincount(eid, length=N_EXPERTS)
    padded = (counts + ROUTE_BLOCK - 1) // ROUTE_BLOCK * ROUTE_BLOCK
    pad_end = jnp.cumsum(padded)
    pad_start = pad_end - padded
    start = jnp.cumsum(counts) - counts
    dest = pad_start[e_s] + jnp.arange(A) - start[e_s]
    n_blocks = -(-(A + N_EXPERTS * (ROUTE_BLOCK - 1)) // ROUTE_BLOCK)
    L = n_blocks * ROUTE_BLOCK
    buf_tok = jnp.zeros((L,), jnp.int32).at[dest].set(tok[order])
    buf_w = jnp.zeros((L,), h.dtype).at[dest].set(wts[order].astype(h.dtype))
    blk_e = jnp.minimum(jnp.searchsorted(pad_end, jnp.arange(n_blocks) * ROUTE_BLOCK, side='right'), N_EXPERTS - 1)

    def expert_block(args):
        tb, wb, e = args
        xb = h[tb]
        yb = (jax.nn.silu(xb @ w1[e]) * (xb @ w3[e])) @ w2[e]
        return yb * wb[:, None]

    ys = lax.map(expert_block, (buf_tok.reshape(n_blocks, ROUTE_BLOCK), buf_w.reshape(n_blocks, ROUTE_BLOCK), blk_e))
    return jax.ops.segment_sum(ys.reshape(L, D), buf_tok, num_segments=N)


def trunk(x, c, w_ada, b_ada, norm1_g, w_in, pool_w, pool_scale, conv_dw_w, conv_dw_b,
          conv_ln_g, conv_ln_b, conv_pw_w, conv_pw_b, fourier_w, w_out, norm2_g,
          router_coarse_w, router_coarse_b, router_fine_w, router_fine_b,
          expert_w1, expert_w3, expert_w2, final_g):
    B, S, D = x.shape
    for l in range(DEPTH):
        mod = jax.nn.silu(c) @ w_ada[l] + b_ada[l]
        sh1, sc1, g1, sh2, sc2, g2 = [m[:, None, :] for m in jnp.split(mod, 6, axis=-1)]
        h = rmsnorm(x, norm1_g[l]) * (1 + sc1) + sh1
        z = h @ w_in[l]
        za = z[..., :POOL_W]
        zb = z[..., POOL_W:POOL_W + 2 * CONV_W]
        zc = z[..., POOL_W + 2 * CONV_W:]
        mixed = jnp.concatenate([
            pool_mixer(za, pool_w[l], pool_scale[l]),
            conv_module(zb, conv_dw_w[l], conv_dw_b[l], conv_ln_g[l], conv_ln_b[l], conv_pw_w[l], conv_pw_b[l]),
            fourier_mixer(zc, fourier_w[l]),
        ], axis=-1)
        x = x + g1 * (mixed @ w_out[l])
        h = rmsnorm(x, norm2_g[l]) * (1 + sc2) + sh2
        y = hier_moe(h.reshape(B * S, D), router_coarse_w[l], router_coarse_b[l], router_fine_w[l],
                     router_fine_b[l], expert_w1[l], expert_w3[l], expert_w2[l])
        x = x + g2 * y.reshape(B, S, D)
    return rmsnorm(x, final_g)


def _normal(k, shape, std):
    return std * jax.random.normal(k, shape, jnp.float32)


def setup_inputs(seed: int = 0) -> dict:
    key = jax.random.key(seed)
    ks = jax.random.split(key, 32)
    D = D_MODEL
    return {
        'x_prompt': _normal(ks[0], (BATCH, SEQ, D), 1.0),
        'x_sample': _normal(ks[1], (DEC_BATCH, DEC_SEQ, D), 1.0),
        'c_prompt': _normal(ks[2], (BATCH, D), 1.0),
        'c_sample': _normal(ks[3], (DEC_BATCH, D), 1.0),
        'w_ada': _normal(ks[4], (DEPTH, D, 6 * D), 0.5 * D ** -0.5),
        'b_ada': _normal(ks[5], (DEPTH, 6 * D), 0.02),
        'norm1_g': 1.0 + _normal(ks[6], (DEPTH, D), 0.02),
        'w_in': _normal(ks[7], (DEPTH, D, IN_COLS), D ** -0.5),
        'pool_w': _normal(ks[8], (DEPTH, N_POOL, POOL_CH, POOL_CH), POOL_CH ** -0.5),
        'pool_scale': 1.0 + _normal(ks[9], (DEPTH, POOL_W), 0.1),
        'conv_dw_w': _normal(ks[10], (DEPTH, CONV_K, CONV_W), CONV_K ** -0.5),
        'conv_dw_b': _normal(ks[11], (DEPTH, CONV_W), 0.02),
        'conv_ln_g': 1.0 + _normal(ks[12], (DEPTH, CONV_W), 0.02),
        'conv_ln_b': _normal(ks[13], (DEPTH, CONV_W), 0.02),
        'conv_pw_w': _normal(ks[14], (DEPTH, CONV_W, CONV_W), CONV_W ** -0.5),
        'conv_pw_b': _normal(ks[15], (DEPTH, CONV_W), 0.02),
        'fourier_w': _normal(ks[16], (DEPTH, FOURIER_W, FOURIER_W), FOURIER_W ** -0.5),
        'w_out': _normal(ks[17], (DEPTH, MIX_W, D), MIX_W ** -0.5),
        'norm2_g': 1.0 + _normal(ks[18], (DEPTH, D), 0.02),
        'router_coarse_w': _normal(ks[19], (DEPTH, D, N_GROUPS), D ** -0.5),
        'router_coarse_b': _normal(ks[20], (DEPTH, N_GROUPS), 0.01),
        'router_fine_w': _normal(ks[21], (DEPTH, N_GROUPS, D, EXPERTS_PER_GROUP), D ** -0.5),
        'router_fine_b': _normal(ks[22], (DEPTH, N_GROUPS, EXPERTS_PER_GROUP), 0.01),
        'expert_w1': _normal(ks[23], (DEPTH, N_EXPERTS, D, D_EXPERT), D ** -0.5),
        'expert_w3': _normal(ks[24], (DEPTH, N_EXPERTS, D, D_EXPERT), D ** -0.5),
        'expert_w2': _normal(ks[25], (DEPTH, N_EXPERTS, D_EXPERT, D), D_EXPERT ** -0.5),
        'final_g': 1.0 + _normal(ks[26], (D,), 0.02),
    }


def reference(x_prompt, x_sample, c_prompt, c_sample, w_ada, b_ada, norm1_g, w_in, pool_w,
              pool_scale, conv_dw_w, conv_dw_b, conv_ln_g, conv_ln_b, conv_pw_w, conv_pw_b,
              fourier_w, w_out, norm2_g, router_coarse_w, router_coarse_b, router_fine_w,
              router_fine_b, expert_w1, expert_w3, expert_w2, final_g):
    params = (w_ada, b_ada, norm1_g, w_in, pool_w, pool_scale, conv_dw_w, conv_dw_b,
              conv_ln_g, conv_ln_b, conv_pw_w, conv_pw_b, fourier_w, w_out, norm2_g,
              router_coarse_w, router_coarse_b, router_fine_w, router_fine_b,
              expert_w1, expert_w3, expert_w2, final_g)
    y_prompt = trunk(x_prompt, c_prompt, *params)
    y_sample = trunk(x_sample, c_sample, *params)
    return (y_prompt, y_sample)
```

```python
import functools
import math

import numpy as np
import jax
import jax.numpy as jnp
from jax import lax
from jax.experimental import pallas as pl
from jax.experimental.pallas import tpu as pltpu

EPS = 1e-6
POOL_WINDOWS = (2, 4, 8, 16)
N_FOURIER = 4
TOP_K = 2
BF = jnp.bfloat16
F32 = jnp.float32

LANES = 128
SUBLANES = 8
BF16_ROWS = 16
VMEM_LIMIT = 56 * 1024 * 1024
ROUTE_ROWS = 256
CONV_ROWS = 64


def _dot(a, b):
    return jnp.dot(a, b, preferred_element_type=F32)


def _params(*sem):
    return pltpu.CompilerParams(dimension_semantics=sem, vmem_limit_bytes=VMEM_LIMIT)


class _Geom:
    def __init__(self, b0, s0, b1, s1, tm):
        assert s0 % tm == 0 and s1 % tm == 0
        self.b0, self.s0, self.b1, self.s1, self.tm = b0, s0, b1, s1, tm
        self.n0, self.n1 = b0 * s0, b1 * s1
        self.n = self.n0 + self.n1
        self.n0t, self.nt = self.n0 // tm, self.n // tm
        self.tps0, self.tps1 = s0 // tm, s1 // tm

    def seq_of_tile(self, i):
        return jnp.where(i < self.n0t, i // self.tps0, self.b0 + (i - self.n0t) // self.tps1)

    def tile_in_seq(self, i):
        tin = jnp.where(i < self.n0t, i % self.tps0, (i - self.n0t) % self.tps1)
        tps = jnp.where(i < self.n0t, self.tps0, self.tps1)
        return tin, tps


def _rmsnorm(x, g):
    return x * lax.rsqrt(jnp.mean(x * x, axis=-1, keepdims=True) + EPS) * g


def _ada_kernel(c_ref, w_ref, b_ref, o_ref):
    c = c_ref[...]
    s = (c * jax.nn.sigmoid(c)).astype(BF)
    o_ref[...] = _dot(s, w_ref[...].astype(BF)) + b_ref[...]


def _ada(c, w_ada, b_ada):
    depth, d, d6 = w_ada.shape
    nb = c.shape[0]
    tn = min(1024, d6)
    return pl.pallas_call(
        _ada_kernel,
        grid=(depth, d6 // tn),
        in_specs=[
            pl.BlockSpec((nb, d), lambda l, j: (0, 0)),
            pl.BlockSpec((None, d, tn), lambda l, j: (l, 0, j)),
            pl.BlockSpec((None, 1, tn), lambda l, j: (l, 0, j)),
        ],
        out_specs=pl.BlockSpec((None, nb, tn), lambda l, j: (l, 0, j)),
        out_shape=jax.ShapeDtypeStruct((depth, nb, d6), F32),
        compiler_params=_params("arbitrary", "arbitrary"),
        name="ada",
    )(c, w_ada, b_ada.reshape(depth, 1, d6))


def _mod_spec(geo, d, j):
    return pl.BlockSpec((None, None, 1, d), lambda i: (geo.seq_of_tile(i), j, 0, 0))


def _two_group_specs(geo, width):
    tm = geo.tm
    return [
        pl.BlockSpec((tm, width), lambda i: (jnp.minimum(i, geo.n0t - 1), 0)),
        pl.BlockSpec((tm, width), lambda i: (jnp.maximum(i - geo.n0t, 0), 0)),
    ]


def _pick_group(geo, ref0, ref1):
    return jnp.where(pl.program_id(0) < geo.n0t, ref0[...], ref1[...])


def _inproj_kernel(*refs, geo, pw, cw, split):
    if split:
        x0_ref, x1_ref, g_ref, sc_ref, sh_ref, w_ref, za_ref, v_ref, zc_ref = refs
        x = _pick_group(geo, x0_ref, x1_ref)
    else:
        x_ref, g_ref, sc_ref, sh_ref, w_ref, za_ref, v_ref, zc_ref = refs
        x = x_ref[...]
    h = _rmsnorm(x, g_ref[...]) * (1.0 + sc_ref[...]) + sh_ref[...]
    hb = h.astype(BF)
    za_ref[...] = _dot(hb, w_ref[:, :pw])
    a = _dot(hb, w_ref[:, pw:pw + cw])
    b = _dot(hb, w_ref[:, pw + cw:pw + 2 * cw])
    v_ref[...] = (a * jax.nn.sigmoid(b)).astype(BF)
    zc_ref[...] = _dot(hb, w_ref[:, pw + 2 * cw:]).astype(BF)


def _inproj(xs, g, mod, w_in, geo, pw, cw, fw):
    d = w_in.shape[0]
    tm = geo.tm
    split = len(xs) == 2
    x_specs = _two_group_specs(geo, d) if split else [pl.BlockSpec((tm, d), lambda i: (i, 0))]
    return pl.pallas_call(
        functools.partial(_inproj_kernel, geo=geo, pw=pw, cw=cw, split=split),
        grid=(geo.nt,),
        in_specs=x_specs + [
            pl.BlockSpec((1, d), lambda i: (0, 0)),
            _mod_spec(geo, d, 1),
            _mod_spec(geo, d, 0),
            pl.BlockSpec(w_in.shape, lambda i: (0, 0), pipeline_mode=pl.Buffered(1)),
        ],
        out_specs=[
            pl.BlockSpec((tm, pw), lambda i: (i, 0)),
            pl.BlockSpec((tm, cw), lambda i: (i, 0)),
            pl.BlockSpec((tm, fw), lambda i: (i, 0)),
        ],
        out_shape=[
            jax.ShapeDtypeStruct((geo.n, pw), F32),
            jax.ShapeDtypeStruct((geo.n, cw), BF),
            jax.ShapeDtypeStruct((geo.n, fw), BF),
        ],
        compiler_params=_params("arbitrary"),
        name="inproj",
    )(*xs, g, mod, mod, w_in)


def _mix_kernel(za_ref, zap_ref, zan_ref, v_ref, vp_ref, vn_ref, pw_ref, ps_ref,
                dw_ref, dwb_ref, lng_ref, lnb_ref, pww_ref, pwb_ref,
                ma_ref, mb_ref, za_s, vs_s, cv_s, *, geo, n_taps):
    tm = geo.tm
    i = pl.program_id(0)
    tin, tps = geo.tile_in_seq(i)
    is_first = tin == 0
    is_last = tin == tps - 1
    pos0 = tin * tm
    seq_len = tps * tm
    ha = SUBLANES
    hv = BF16_ROWS
    half = n_taps // 2

    za = za_ref[...]
    za_s[0:ha, :] = jnp.where(is_first, 0.0, zap_ref[...])
    za_s[ha:ha + tm, :] = za
    za_s[ha + tm:, :] = jnp.where(is_last, 0.0, zan_ref[...])
    t = pos0 + lax.broadcasted_iota(jnp.int32, (tm, 1), 0)
    n_pool = pw_ref.shape[0]
    pc = pw_ref.shape[1]
    for gi in range(n_pool):
        w = POOL_WINDOWS[gi]
        lo, hi = -(w // 2), w - w // 2
        c0 = gi * pc
        acc = za_s[ha + lo:ha + lo + tm, c0:c0 + pc]
        for o in range(lo + 1, hi):
            acc = acc + za_s[ha + o:ha + o + tm, c0:c0 + pc]
        cnt = (jnp.minimum(t + hi, seq_len) - jnp.maximum(t + lo, 0)).astype(F32)
        dlt = acc / cnt - za[:, c0:c0 + pc]
        y = _dot(dlt.astype(BF), pw_ref[gi]) * ps_ref[:, c0:c0 + pc]
        ma_ref[:, c0:c0 + pc] = y.astype(BF)

    cw = v_ref.shape[1]
    rows = tm + 2 * hv
    vs_s[0, 0:hv, :] = jnp.where(is_first, 0.0, vp_ref[...].astype(F32))
    vs_s[0, hv:hv + tm, :] = v_ref[...].astype(F32)
    vs_s[0, hv + tm:rows, :] = jnp.where(is_last, 0.0, vn_ref[...].astype(F32))
    shifted = rows - SUBLANES
    for r in range(1, SUBLANES):
        vs_s[r, 0:shifted, :] = vs_s[0, r:r + shifted, :]

    base = hv - half

    def conv_rows(rb, carry):
        r0 = pl.multiple_of(rb * CONV_ROWS, CONV_ROWS)
        for cb in range(cw // LANES):
            c0 = cb * LANES
            acc = jnp.broadcast_to(dwb_ref[:, c0:c0 + LANES], (CONV_ROWS, LANES))
            for k in range(n_taps):
                q, r = divmod(k + base, SUBLANES)
                win = vs_s[r, pl.ds(r0 + q * SUBLANES, CONV_ROWS), c0:c0 + LANES]
                acc = acc + win * dw_ref[k:k + 1, c0:c0 + LANES]
            cv_s[pl.ds(r0, CONV_ROWS), c0:c0 + LANES] = acc
        return carry

    lax.fori_loop(0, tm // CONV_ROWS, conv_rows, 0)

    u = cv_s[...]
    uc = u - jnp.mean(u, axis=-1, keepdims=True)
    var = jnp.mean(uc * uc, axis=-1, keepdims=True)
    y = uc * lax.rsqrt(var + EPS) * lng_ref[...] + lnb_ref[...]
    y = y * jax.nn.sigmoid(y)
    mb_ref[...] = (_dot(y.astype(BF), pww_ref[...]) + pwb_ref[...]).astype(BF)


def _mix(za, v, pool_w, pool_scale, dw_w, dw_b, ln_g, ln_b, pw_w, pw_b, geo):
    tm = geo.tm
    n, pw = za.shape
    cw = v.shape[1]
    n_taps = dw_w.shape[0]
    assert n_taps // 2 < BF16_ROWS and max(POOL_WINDOWS) // 2 <= SUBLANES
    assert tm % CONV_ROWS == 0 and tm % BF16_ROWS == 0
    ra, rv = tm // SUBLANES, tm // BF16_ROWS
    full = lambda a: pl.BlockSpec(a.shape, lambda i: (0,) * a.ndim)
    return pl.pallas_call(
        functools.partial(_mix_kernel, geo=geo, n_taps=n_taps),
        grid=(geo.nt,),
        in_specs=[
            pl.BlockSpec((tm, pw), lambda i: (i, 0)),
            pl.BlockSpec((SUBLANES, pw), lambda i: (jnp.maximum(i * ra - 1, 0), 0)),
            pl.BlockSpec((SUBLANES, pw), lambda i: (jnp.minimum((i + 1) * ra, n // SUBLANES - 1), 0)),
            pl.BlockSpec((tm, cw), lambda i: (i, 0)),
            pl.BlockSpec((BF16_ROWS, cw), lambda i: (jnp.maximum(i * rv - 1, 0), 0)),
            pl.BlockSpec((BF16_ROWS, cw), lambda i: (jnp.minimum((i + 1) * rv, n // BF16_ROWS - 1), 0)),
            full(pool_w), full(pool_scale), full(dw_w), full(dw_b), full(ln_g), full(ln_b),
            full(pw_w), full(pw_b),
        ],
        out_specs=[
            pl.BlockSpec((tm, pw), lambda i: (i, 0)),
            pl.BlockSpec((tm, cw), lambda i: (i, 0)),
        ],
        out_shape=[jax.ShapeDtypeStruct((n, pw), BF), jax.ShapeDtypeStruct((n, cw), BF)],
        scratch_shapes=[
            pltpu.VMEM((tm + 2 * SUBLANES, pw), F32),
            pltpu.VMEM((SUBLANES, tm + 2 * BF16_ROWS, cw), F32),
            pltpu.VMEM((tm, cw), F32),
        ],
        compiler_params=_params("arbitrary"),
        name="mix",
    )(za, za, za, v, v, v, pool_w, pool_scale, dw_w, dw_b, ln_g, ln_b, pw_w, pw_b)


def _dft_split(s):
    so = 1 << ((int(math.log2(s)) + 1) // 2)
    so = max(min(so, 128), 8)
    assert s % so == 0
    return so, s // so


def _dft_tables(s, so, si, ch):
    k = np.arange(so)
    ang1 = 2.0 * np.pi * ((k[:, None] * k[None, :]) % so) / so
    f1 = np.concatenate([np.cos(ang1), -np.sin(ang1)], axis=0)
    ka = np.arange(so)[:, None, None]
    kb = np.arange(si)[None, :, None]
    ni = np.arange(si)[None, None, :]
    num = (ni * kb * so + ni * ka) % s
    ang2 = 2.0 * np.pi * num / s
    gr = np.cos(ang2) / np.sqrt(s)
    gi = -np.sin(ang2) / np.sqrt(s)
    g = np.concatenate([np.concatenate([gr, -gi], axis=2),
                        np.concatenate([gi, gr], axis=2)], axis=1)
    c = np.arange(ch)
    angc = 2.0 * np.pi * ((c[:, None] * c[None, :]) % ch) / ch
    eye = np.eye(N_FOURIER)
    cbd = np.kron(eye, np.cos(angc) / np.sqrt(ch))
    sbd = np.kron(eye, np.sin(angc) / np.sqrt(ch))
    as_bf = lambda a: jnp.asarray(a.astype(np.float32)).astype(BF)
    return as_bf(f1), as_bf(g), as_bf(cbd), as_bf(sbd)


def _dft1_kernel(f_ref, z_ref, ar_ref, ai_ref, *, so):
    r = _dot(f_ref[...], z_ref[...])
    ar_ref[...] = r[:so].astype(BF)
    ai_ref[...] = r[so:].astype(BF)


def _dft2_kernel(g_ref, ar_ref, ai_ref, cbd_ref, sbd_ref, fw_ref, o_ref, zr_s, zi_s, *, ka, si, w):
    for j in range(ka):
        a = jnp.concatenate([ar_ref[j * si:(j + 1) * si, :], ai_ref[j * si:(j + 1) * si, :]], axis=0)
        z = _dot(g_ref[j], a)
        zr_s[j * si:(j + 1) * si, :] = z[:si].astype(BF)
        zi_s[j * si:(j + 1) * si, :] = z[si:].astype(BF)
    rz = _dot(zr_s[...], cbd_ref[...]) + _dot(zi_s[...], sbd_ref[...])
    out = _dot(rz.astype(BF), fw_ref[...]).astype(BF)
    for j in range(ka):
        o_ref[:, j * w:(j + 1) * w] = out[j * si:(j + 1) * si, :]


def _fourier(zc, fw, nb, s, row_off):
    n, w = zc.shape
    so, si = _dft_split(s)
    assert row_off % s == 0 and si % BF16_ROWS == 0
    f1, g, cbd, sbd = _dft_tables(s, so, si, w // N_FOURIER)
    blk_off = row_off // s
    cols = si * w
    tn = min(cols, 4096)
    ar, ai = pl.pallas_call(
        functools.partial(_dft1_kernel, so=so),
        grid=(nb, cols // tn),
        in_specs=[
            pl.BlockSpec(f1.shape, lambda b, j: (0, 0)),
            pl.BlockSpec((so, tn), lambda b, j: (blk_off + b, j)),
        ],
        out_specs=[pl.BlockSpec((so, tn), lambda b, j: (b, j))] * 2,
        out_shape=[jax.ShapeDtypeStruct((nb * so, cols), BF)] * 2,
        compiler_params=_params("arbitrary", "arbitrary"),
        name="dft1",
    )(f1, zc.reshape(n // si, cols))
    ka = min(so, max(1, 1024 // si))
    full = lambda a: pl.BlockSpec(a.shape, lambda t, b: (0,) * a.ndim)
    kt = so // ka
    out = pl.pallas_call(
        functools.partial(_dft2_kernel, ka=ka, si=si, w=w),
        grid=(kt, nb),
        in_specs=[
            pl.BlockSpec((ka, 2 * si, 2 * si), lambda t, b: (t, 0, 0)),
            pl.BlockSpec((ka * si, w), lambda t, b: (b * kt + t, 0)),
            pl.BlockSpec((ka * si, w), lambda t, b: (b * kt + t, 0)),
            full(cbd), full(sbd), full(fw),
        ],
        out_specs=pl.BlockSpec((si, ka * w), lambda t, b: (b, t)),
        out_shape=jax.ShapeDtypeStruct((nb * si, so * w), BF),
        scratch_shapes=[pltpu.VMEM((ka * si, w), BF), pltpu.VMEM((ka * si, w), BF)],
        compiler_params=_params("arbitrary", "arbitrary"),
        name="dft2",
    )(g, ar.reshape(nb * s, w), ai.reshape(nb * s, w), cbd, sbd, fw)
    return out.reshape(nb * s, w)


def _outproj_kernel(*refs, geo, pw, cw, split):
    if split:
        x0_ref, x1_ref = refs[:2]
        x = _pick_group(geo, x0_ref, x1_ref)
        refs = refs[2:]
    else:
        x = refs[0][...]
        refs = refs[1:]
    (ma_ref, mb_ref, mc0_ref, mc1_ref, wo_ref, g1_ref, ng_ref, sc_ref, sh_ref,
     wrh_ref, wrl_ref, rb_ref, xo_ref, h_ref, lg_ref) = refs
    mc = _pick_group(geo, mc0_ref, mc1_ref)
    y = _dot(ma_ref[...], wo_ref[0:pw, :])
    y = y + _dot(mb_ref[...], wo_ref[pw:pw + cw, :])
    y = y + _dot(mc, wo_ref[pw + cw:, :])
    x = x + g1_ref[...] * y
    xo_ref[...] = x
    h = _rmsnorm(x, ng_ref[...]) * (1.0 + sc_ref[...]) + sh_ref[...]
    h_ref[...] = h
    hh = h.astype(BF)
    hl = (h - hh.astype(F32)).astype(BF)
    lg = _dot(hh, wrh_ref[...]) + _dot(hl, wrh_ref[...]) + _dot(hh, wrl_ref[...])
    lg_ref[...] = lg + rb_ref[...]


def _outproj(xs, ma, mb, mc0, mc1, w_out, mod, ng, wrh, wrl, rb, geo):
    d = w_out.shape[1]
    tm = geo.tm
    pw, cw = ma.shape[1], mb.shape[1]
    fw = mc0.shape[1]
    split = len(xs) == 2
    x_specs = _two_group_specs(geo, d) if split else [pl.BlockSpec((tm, d), lambda i: (i, 0))]
    full = lambda a: pl.BlockSpec(a.shape, lambda i: (0,) * a.ndim)
    return pl.pallas_call(
        functools.partial(_outproj_kernel, geo=geo, pw=pw, cw=cw, split=split),
        grid=(geo.nt,),
        in_specs=x_specs + [
            pl.BlockSpec((tm, pw), lambda i: (i, 0)),
            pl.BlockSpec((tm, cw), lambda i: (i, 0)),
        ] + _two_group_specs(geo, fw) + [
            pl.BlockSpec(w_out.shape, lambda i: (0, 0), pipeline_mode=pl.Buffered(1)),
            _mod_spec(geo, d, 2),
            full(ng),
            _mod_spec(geo, d, 4),
            _mod_spec(geo, d, 3),
            full(wrh), full(wrl), full(rb),
        ],
        out_specs=[
            pl.BlockSpec((tm, d), lambda i: (i, 0)),
            pl.BlockSpec((tm, d), lambda i: (i, 0)),
            pl.BlockSpec((tm, LANES), lambda i: (i, 0)),
        ],
        out_shape=[
            jax.ShapeDtypeStruct((geo.n, d), F32),
            jax.ShapeDtypeStruct((geo.n, d), F32),
            jax.ShapeDtypeStruct((geo.n, LANES), F32),
        ],
        compiler_params=_params("arbitrary"),
        name="outproj",
    )(*xs, ma, mb, mc0, mc1, w_out, mod, ng, mod, mod, wrh, wrl, rb)


def _route_kernel(lg_ref, info_ref, dest_ref, cnt_ref, carry_s, *, n_exp, n_grp, blk):
    p = pl.program_id(0)
    i = pl.program_id(1)
    tm = lg_ref.shape[0]
    epg = n_exp // n_grp

    @pl.when((p == 0) & (i == 0))
    def _():
        carry_s[...] = jnp.zeros_like(carry_s)

    @pl.when((p == 1) & (i == 0))
    def _():
        cnt = carry_s[...]
        cnt_ref[...] = cnt
        nblk = jnp.floor((cnt + float(blk - 1)) * (1.0 / blk))
        hi = jnp.floor(nblk * (1.0 / 256.0))
        lo = nblk - 256.0 * hi
        r = lax.broadcasted_iota(jnp.int32, (LANES, LANES), 0)
        c = lax.broadcasted_iota(jnp.int32, (LANES, LANES), 1)
        upper = jnp.where(r < c, 1.0, 0.0).astype(BF)
        start_blk = 256.0 * _dot(hi.astype(BF), upper) + _dot(lo.astype(BF), upper)
        carry_s[...] = start_blk * float(blk)

    lg = lg_ref[...]
    lane = lax.broadcasted_iota(jnp.int32, (tm, LANES), 1).astype(F32)
    big = float(1 << 20)
    neg = -jnp.inf
    is_c = (lane >= n_exp) & (lane < n_exp + n_grp)
    lc = jnp.where(is_c, lg, neg)
    m = jnp.max(lc, axis=1, keepdims=True)
    gidx = jnp.min(jnp.where(lc == m, lane - n_exp, big), axis=1, keepdims=True)
    se = jnp.sum(jnp.where(is_c, jnp.exp(lc - m), 0.0), axis=1, keepdims=True)
    p_grp = 1.0 / se
    in_g = (lane >= gidx * epg) & (lane < (gidx + 1.0) * epg)
    lf = jnp.where(in_g, lg, neg)
    v1 = jnp.max(lf, axis=1, keepdims=True)
    i1 = jnp.min(jnp.where((lf == v1) & in_g, lane, big), axis=1, keepdims=True)
    in_g2 = in_g & (lane != i1)
    lf2 = jnp.where(in_g2, lg, neg)
    v2 = jnp.max(lf2, axis=1, keepdims=True)
    i2 = jnp.min(jnp.where((lf2 == v2) & in_g2, lane, big), axis=1, keepdims=True)
    e2 = jnp.exp(v2 - v1)
    den = 1.0 + e2
    w1 = p_grp * (1.0 / den)
    w2 = p_grp * (e2 / den)

    sel1 = lane == i1
    sel2 = lane == i2
    oh = jnp.where(sel1 | sel2, 1.0, 0.0)
    rr = lax.broadcasted_iota(jnp.int32, (tm, tm), 0)
    cc = lax.broadcasted_iota(jnp.int32, (tm, tm), 1)
    lower = jnp.where(cc < rr, 1.0, 0.0).astype(BF)
    slot = carry_s[0:1, :] + _dot(lower, oh.astype(BF))
    d1 = jnp.sum(jnp.where(sel1, slot, 0.0), axis=1, keepdims=True)
    d2 = jnp.sum(jnp.where(sel2, slot, 0.0), axis=1, keepdims=True)
    carry_s[...] = carry_s[...] + jnp.sum(oh, axis=0, keepdims=True)

    @pl.when(p == 1)
    def _():
        info_ref[...] = jnp.where(lane == 0, w1, jnp.where(lane == 1, w2, 0.0))
        dd = jnp.where(lane == 0, d1, jnp.where(lane == 1, d2, 0.0))
        dest_ref[...] = jnp.transpose(dd)[0:SUBLANES, :].astype(jnp.int32)


def _route(lg, n_exp, n_grp, blk, tm):
    n = lg.shape[0]
    nt = n // tm
    return pl.pallas_call(
        functools.partial(_route_kernel, n_exp=n_exp, n_grp=n_grp, blk=blk),
        grid=(2, nt),
        in_specs=[pl.BlockSpec((tm, LANES), lambda p, i: (i, 0))],
        out_specs=[
            pl.BlockSpec((tm, LANES), lambda p, i: (i * p, 0)),
            pl.BlockSpec((None, SUBLANES, tm), lambda p, i: (i * p, 0, 0)),
            pl.BlockSpec((SUBLANES, LANES), lambda p, i: (0, 0)),
        ],
        out_shape=[
            jax.ShapeDtypeStruct((n, LANES), F32),
            jax.ShapeDtypeStruct((nt, SUBLANES, tm), jnp.int32),
            jax.ShapeDtypeStruct((SUBLANES, LANES), F32),
        ],
        scratch_shapes=[pltpu.VMEM((SUBLANES, LANES), F32)],
        compiler_params=_params("arbitrary", "arbitrary"),
        name="route",
    )(lg)


def _dispatch_kernel(cnt_ref, pst_ref, nu_ref, dest_hbm, h_ref, xs_hbm, dsm, isem, sem, zblk, *, blk, n_exp):
    i = pl.program_id(0)
    tm = h_ref.shape[0]
    cp = pltpu.make_async_copy(dest_hbm.at[i], dsm, isem)
    cp.start()
    cp.wait()

    def row_copy(t, k):
        return pltpu.make_async_copy(h_ref.at[pl.ds(t, 1), :], xs_hbm.at[pl.ds(dsm[k, t], 1), :], sem)

    def issue(t, c):
        for k in range(TOP_K):
            row_copy(t, k).start()
        return c

    lax.fori_loop(0, tm, issue, 0)

    def drain(t, c):
        for k in range(TOP_K):
            row_copy(t, k).wait()
        return c

    lax.fori_loop(0, tm, drain, 0)

    @pl.when(i == pl.num_programs(0) - 1)
    def _():
        zblk[...] = jnp.zeros_like(zblk)

        def per_expert(e, c):
            cnt = cnt_ref[e]
            first = pst_ref[e] + cnt
            n_pad = (blk - cnt % blk) % blk

            def pad_row(r, c2):
                z = pltpu.make_async_copy(zblk.at[pl.ds(0, 1), :], xs_hbm.at[pl.ds(first + r, 1), :], sem)
                z.start()
                z.wait()
                return c2

            lax.fori_loop(0, n_pad, pad_row, 0)
            return c

        lax.fori_loop(0, n_exp, per_expert, 0)

        def unused_block(b, c):
            z = pltpu.make_async_copy(zblk, xs_hbm.at[pl.ds(pl.multiple_of(b * blk, blk), blk), :], sem)
            z.start()
            z.wait()
            return c

        lax.fori_loop(nu_ref[0], xs_hbm.shape[0] // blk, unused_block, 0)


def _dispatch(cnt, pstart, n_used, dest, h, n_rows, blk, tm):
    n, d = h.shape
    n_exp = cnt.shape[0]
    return pl.pallas_call(
        functools.partial(_dispatch_kernel, blk=blk, n_exp=n_exp),
        grid_spec=pltpu.PrefetchScalarGridSpec(
            num_scalar_prefetch=3,
            grid=(n // tm,),
            in_specs=[
                pl.BlockSpec(memory_space=pl.ANY),
                pl.BlockSpec((tm, d), lambda i, c, p, u: (i, 0)),
            ],
            out_specs=pl.BlockSpec(memory_space=pl.ANY),
            scratch_shapes=[
                pltpu.SMEM((SUBLANES, tm), jnp.int32),
                pltpu.SemaphoreType.DMA,
                pltpu.SemaphoreType.DMA,
                pltpu.VMEM((blk, d), F32),
            ],
        ),
        out_shape=jax.ShapeDtypeStruct((n_rows, d), F32),
        compiler_params=_params("arbitrary"),
        name="dispatch",
    )(cnt, pstart, n_used, dest, h)


def _expert_kernel(be_ref, bi_ref, nu_ref, xs_ref, w1_ref, w3_ref, w2_ref, ys_ref):
    used = pl.program_id(0) < nu_ref[0]

    @pl.when(jnp.logical_not(used))
    def _():
        ys_ref[...] = jnp.zeros_like(ys_ref)

    @pl.when(used)
    def _():
        x = xs_ref[...].astype(BF)
        a = _dot(x, w1_ref[...])
        c = _dot(x, w3_ref[...])
        hm = (a * jax.nn.sigmoid(a) * c).astype(BF)
        ys_ref[...] = _dot(hm, w2_ref[...])


def _experts(blk_e, blk_i, n_used, xs, w1, w3, w2, blk):
    n_rows, d = xs.shape
    de = w1.shape[2]
    return pl.pallas_call(
        _expert_kernel,
        grid_spec=pltpu.PrefetchScalarGridSpec(
            num_scalar_prefetch=3,
            grid=(n_rows // blk,),
            in_specs=[
                pl.BlockSpec((blk, d), lambda b, be, bi, nu: (bi[b], 0)),
                pl.BlockSpec((None, d, de), lambda b, be, bi, nu: (be[b], 0, 0)),
                pl.BlockSpec((None, d, de), lambda b, be, bi, nu: (be[b], 0, 0)),
                pl.BlockSpec((None, de, d), lambda b, be, bi, nu: (be[b], 0, 0)),
            ],
            out_specs=pl.BlockSpec((blk, d), lambda b, be, bi, nu: (b, 0)),
        ),
        out_shape=jax.ShapeDtypeStruct((n_rows, d), F32),
        compiler_params=_params("arbitrary"),
        name="experts",
    )(blk_e, blk_i, n_used, xs, w1, w3, w2)


def _combine_kernel(*refs, geo, final):
    if final:
        dest_hbm, ys_hbm, x_ref, info_ref, g2_ref, fg_ref, o0_ref, o1_ref, dsm, isem, buf, sem = refs
    else:
        dest_hbm, ys_hbm, x_ref, info_ref, g2_ref, o_ref, dsm, isem, buf, sem = refs
    i = pl.program_id(0)
    tm = x_ref.shape[0]
    cp = pltpu.make_async_copy(dest_hbm.at[i], dsm, isem)
    cp.start()
    cp.wait()

    def row_copy(t, k):
        return pltpu.make_async_copy(ys_hbm.at[pl.ds(dsm[k, t], 1), :], buf.at[k, pl.ds(t, 1), :], sem)

    def issue(t, c):
        for k in range(TOP_K):
            row_copy(t, k).start()
        return c

    lax.fori_loop(0, tm, issue, 0)

    def drain(t, c):
        for k in range(TOP_K):
            row_copy(t, k).wait()
        return c

    lax.fori_loop(0, tm, drain, 0)

    y = buf[0] * info_ref[:, 0:1] + buf[1] * info_ref[:, 1:2]
    x = x_ref[...] + g2_ref[...] * y
    if final:
        x = _rmsnorm(x, fg_ref[...])

        @pl.when(i < geo.n0t)
        def _():
            o0_ref[...] = x

        @pl.when(i >= geo.n0t)
        def _():
            o1_ref[...] = x
    else:
        o_ref[...] = x


def _combine(dest, ys, x, info, mod, geo, final_g=None):
    n, d = x.shape
    tm = geo.tm
    final = final_g is not None
    in_specs = [
        pl.BlockSpec(memory_space=pl.ANY),
        pl.BlockSpec(memory_space=pl.ANY),
        pl.BlockSpec((tm, d), lambda i: (i, 0)),
        pl.BlockSpec((tm, LANES), lambda i: (i, 0)),
        _mod_spec(geo, d, 5),
    ]
    args = [dest, ys, x, info, mod]
    if final:
        in_specs.append(pl.BlockSpec((1, d), lambda i: (0, 0)))
        args.append(final_g)
        out_specs = _two_group_specs(geo, d)
        out_shape = [jax.ShapeDtypeStruct((geo.n0, d), F32), jax.ShapeDtypeStruct((geo.n1, d), F32)]
    else:
        out_specs = pl.BlockSpec((tm, d), lambda i: (i, 0))
        out_shape = jax.ShapeDtypeStruct((n, d), F32)
    return pl.pallas_call(
        functools.partial(_combine_kernel, geo=geo, final=final),
        grid=(geo.nt,),
        in_specs=in_specs,
        out_specs=out_specs,
        out_shape=out_shape,
        scratch_shapes=[
            pltpu.SMEM((SUBLANES, tm), jnp.int32),
            pltpu.SemaphoreType.DMA,
            pltpu.VMEM((TOP_K, tm, d), F32),
            pltpu.SemaphoreType.DMA,
        ],
        compiler_params=_params("arbitrary"),
        name="combine",
    )(*args)


def _block_tables(cnt_f, n_exp, blk, n_blocks):
    cnt = cnt_f[0, :n_exp].astype(jnp.int32)
    nblk = (cnt + (blk - 1)) // blk
    end_blk = jnp.cumsum(nblk)
    n_used = end_blk[-1]
    pstart = (end_blk - nblk) * blk
    b = jnp.minimum(jnp.arange(n_blocks, dtype=jnp.int32), n_used - 1)
    blk_e = jnp.minimum(jnp.searchsorted(end_blk, b, side="right"), n_exp - 1).astype(jnp.int32)
    return cnt, pstart.astype(jnp.int32), blk_e, b, n_used.reshape(1).astype(jnp.int32)


def kernel(x_prompt, x_sample, c_prompt, c_sample, w_ada, b_ada, norm1_g, w_in, pool_w, pool_scale, conv_dw_w, conv_dw_b, conv_ln_g, conv_ln_b, conv_pw_w, conv_pw_b, fourier_w, w_out, norm2_g, router_coarse_w, router_coarse_b, router_fine_w, router_fine_b, expert_w1, expert_w3, expert_w2, final_g):
    b0, s0, d = x_prompt.shape
    b1, s1, _ = x_sample.shape
    depth = w_ada.shape[0]
    pw = pool_w.shape[1] * pool_w.shape[2]
    cw = conv_pw_w.shape[1]
    fw = fourier_w.shape[1]
    n_grp = router_coarse_w.shape[2]
    epg = router_fine_w.shape[3]
    n_exp = n_grp * epg
    assert n_exp + n_grp <= LANES and pool_w.shape[1] == len(POOL_WINDOWS)
    tm = math.gcd(math.gcd(s0, s1), 256)
    geo = _Geom(b0, s0, b1, s1, tm)
    n = geo.n
    n_seq = b0 + b1
    blk = ROUTE_ROWS
    n_blocks = -(-(n * TOP_K + n_exp * (blk - 1)) // blk)

    c = jnp.concatenate([c_prompt, c_sample], axis=0)
    mod = _ada(c, w_ada, b_ada).reshape(depth, n_seq, 6, 1, d)

    xs = [x_prompt.reshape(geo.n0, d), x_sample.reshape(geo.n1, d)]
    for l in range(depth):
        ml = mod[l]
        za, v, zc = _inproj(xs, norm1_g[l].reshape(1, d), ml, w_in[l].astype(BF), geo, pw, cw, fw)
        ma, mb = _mix(za, v, pool_w[l].astype(BF), pool_scale[l].reshape(1, pw), conv_dw_w[l],
                      conv_dw_b[l].reshape(1, cw), conv_ln_g[l].reshape(1, cw), conv_ln_b[l].reshape(1, cw),
                      conv_pw_w[l].astype(BF), conv_pw_b[l].reshape(1, cw), geo)
        fwl = fourier_w[l].astype(BF)
        mc0 = _fourier(zc, fwl, b0, s0, 0)
        mc1 = _fourier(zc, fwl, b1, s1, geo.n0)

        wr = jnp.concatenate([
            jnp.transpose(router_fine_w[l], (1, 0, 2)).reshape(d, n_exp),
            router_coarse_w[l],
            jnp.zeros((d, LANES - n_exp - n_grp), F32)], axis=1)
        wrh = wr.astype(BF)
        wrl = (wr - wrh.astype(F32)).astype(BF)
        rb = jnp.concatenate([router_fine_b[l].reshape(n_exp), router_coarse_b[l],
                              jnp.zeros((LANES - n_exp - n_grp,), F32)]).reshape(1, LANES)
        x, h, lg = _outproj(xs, ma, mb, mc0, mc1, w_out[l].astype(BF), ml, norm2_g[l].reshape(1, d),
                            wrh, wrl, rb, geo)

        info, dest, cnt_f = _route(lg, n_exp, n_grp, blk, tm)
        cnt, pstart, blk_e, blk_i, n_used = _block_tables(cnt_f, n_exp, blk, n_blocks)
        xsort = _dispatch(cnt, pstart, n_used, dest, h, n_blocks * blk, blk, tm)
        ys = _experts(blk_e, blk_i, n_used, xsort, expert_w1[l].astype(BF), expert_w3[l].astype(BF),
                      expert_w2[l].astype(BF), blk)
        if l + 1 < depth:
            xs = [_combine(dest, ys, x, info, ml, geo)]
        else:
            y0, y1 = _combine(dest, ys, x, info, ml, geo, final_g.reshape(1, d))
    return (y0.reshape(b0, s0, d), y1.reshape(b1, s1, d))
```

```python
import functools
import math

import numpy as np
import jax
import jax.numpy as jnp
from jax import lax
from jax.experimental import pallas as pl
from jax.experimental.pallas import tpu as pltpu

EPS = 1e-6
POOL_WINDOWS = (2, 4, 8, 16)
N_FOURIER = 4
TOP_K = 2
BF = jnp.bfloat16
F32 = jnp.float32

LANES = 128
SUBLANES = 8
BF16_ROWS = 16
VMEM_LIMIT = 56 * 1024 * 1024
ROUTE_ROWS = 256
CONV_ROWS = 64
DMA_UNROLL = 8


def _dot(a, b):
    return jnp.dot(a, b, preferred_element_type=F32)


def _params(*sem):
    return pltpu.CompilerParams(dimension_semantics=sem, vmem_limit_bytes=VMEM_LIMIT)


class _Geom:
    def __init__(self, b0, s0, b1, s1, tm):
        assert s0 % tm == 0 and s1 % tm == 0
        self.b0, self.s0, self.b1, self.s1, self.tm = b0, s0, b1, s1, tm
        self.n0, self.n1 = b0 * s0, b1 * s1
        self.n = self.n0 + self.n1
        self.n0t, self.nt = self.n0 // tm, self.n // tm
        self.tps0, self.tps1 = s0 // tm, s1 // tm

    def seq_of_tile(self, i):
        return jnp.where(i < self.n0t, i // self.tps0, self.b0 + (i - self.n0t) // self.tps1)

    def tile_in_seq(self, i):
        tin = jnp.where(i < self.n0t, i % self.tps0, (i - self.n0t) % self.tps1)
        tps = jnp.where(i < self.n0t, self.tps0, self.tps1)
        return tin, tps


def _rmsnorm(x, g):
    return x * lax.rsqrt(jnp.mean(x * x, axis=-1, keepdims=True) + EPS) * g


HI_MASK = 0xFFFF0000


def _slab_rows(d):
    return d // (2 * LANES)


def _pack_rows(x, slab_ref):
    rows, d = x.shape
    half, ns = d // 2, _slab_rows(d)
    bits = pltpu.bitcast(x.astype(BF).astype(F32), jnp.uint32)
    words = (bits[:, :half] >> 16) | (bits[:, half:] & jnp.uint32(HI_MASK))
    for s in range(ns):
        slab_ref[pl.ds(s, rows, stride=ns), :] = words[:, s * LANES:(s + 1) * LANES]


def _unpack_rows(slab_ref, rows, dtype):
    ns = slab_ref.shape[0] // rows
    lo, hi = [], []
    for s in range(ns):
        words = slab_ref[pl.ds(s, rows, stride=ns), :]
        lo.append(pltpu.bitcast(words << 16, F32).astype(dtype))
        hi.append(pltpu.bitcast(words & jnp.uint32(HI_MASK), F32).astype(dtype))
    return jnp.concatenate(lo + hi, axis=1)


def _slab_at(ref, row, ns):
    return ref.at[pl.ds(pl.multiple_of(row * ns, ns), ns), :]


def _ada_kernel(c_ref, w_ref, b_ref, o_ref):
    c = c_ref[...]
    s = (c * jax.nn.sigmoid(c)).astype(BF)
    o_ref[...] = _dot(s, w_ref[...].astype(BF)) + b_ref[...]


def _ada(c, w_ada, b_ada):
    depth, d, d6 = w_ada.shape
    nb = c.shape[0]
    tn = min(1024, d6)
    return pl.pallas_call(
        _ada_kernel,
        grid=(depth, d6 // tn),
        in_specs=[
            pl.BlockSpec((nb, d), lambda l, j: (0, 0)),
            pl.BlockSpec((None, d, tn), lambda l, j: (l, 0, j)),
            pl.BlockSpec((None, 1, tn), lambda l, j: (l, 0, j)),
        ],
        out_specs=pl.BlockSpec((None, nb, tn), lambda l, j: (l, 0, j)),
        out_shape=jax.ShapeDtypeStruct((depth, nb, d6), F32),
        compiler_params=_params("arbitrary", "arbitrary"),
        name="ada",
    )(c, w_ada, b_ada.reshape(depth, 1, d6))


def _mod_spec(geo, d, j):
    return pl.BlockSpec((None, None, 1, d), lambda i: (geo.seq_of_tile(i), j, 0, 0))


def _two_group_specs(geo, width):
    tm = geo.tm
    return [
        pl.BlockSpec((tm, width), lambda i: (jnp.minimum(i, geo.n0t - 1), 0)),
        pl.BlockSpec((tm, width), lambda i: (jnp.maximum(i - geo.n0t, 0), 0)),
    ]


def _pick_group(geo, ref0, ref1):
    return jnp.where(pl.program_id(0) < geo.n0t, ref0[...], ref1[...])


def _inproj_kernel(*refs, geo, pw, cw, split):
    if split:
        x0_ref, x1_ref, g_ref, sc_ref, sh_ref, w_ref, za_ref, v_ref, zc_ref = refs
        x = _pick_group(geo, x0_ref, x1_ref)
    else:
        x_ref, g_ref, sc_ref, sh_ref, w_ref, za_ref, v_ref, zc_ref = refs
        x = x_ref[...]
    h = _rmsnorm(x, g_ref[...]) * (1.0 + sc_ref[...]) + sh_ref[...]
    hb = h.astype(BF)
    za_ref[...] = _dot(hb, w_ref[:, :pw])
    a = _dot(hb, w_ref[:, pw:pw + cw])
    b = _dot(hb, w_ref[:, pw + cw:pw + 2 * cw])
    v_ref[...] = (a * jax.nn.sigmoid(b)).astype(BF)
    zc_ref[...] = _dot(hb, w_ref[:, pw + 2 * cw:]).astype(BF)


def _inproj(xs, g, mod, w_in, geo, pw, cw, fw):
    d = w_in.shape[0]
    tm = geo.tm
    split = len(xs) == 2
    x_specs = _two_group_specs(geo, d) if split else [pl.BlockSpec((tm, d), lambda i: (i, 0))]
    return pl.pallas_call(
        functools.partial(_inproj_kernel, geo=geo, pw=pw, cw=cw, split=split),
        grid=(geo.nt,),
        in_specs=x_specs + [
            pl.BlockSpec((1, d), lambda i: (0, 0)),
            _mod_spec(geo, d, 1),
            _mod_spec(geo, d, 0),
            pl.BlockSpec(w_in.shape, lambda i: (0, 0), pipeline_mode=pl.Buffered(1)),
        ],
        out_specs=[
            pl.BlockSpec((tm, pw), lambda i: (i, 0)),
            pl.BlockSpec((tm, cw), lambda i: (i, 0)),
            pl.BlockSpec((tm, fw), lambda i: (i, 0)),
        ],
        out_shape=[
            jax.ShapeDtypeStruct((geo.n, pw), F32),
            jax.ShapeDtypeStruct((geo.n, cw), BF),
            jax.ShapeDtypeStruct((geo.n, fw), BF),
        ],
        compiler_params=_params("arbitrary"),
        name="inproj",
    )(*xs, g, mod, mod, w_in)


def _mix_kernel(za_ref, zap_ref, zan_ref, v_ref, vp_ref, vn_ref, pw_ref, ps_ref,
                dw_ref, dwb_ref, lng_ref, lnb_ref, pww_ref, pwb_ref,
                ma_ref, mb_ref, za_s, vs_s, cv_s, *, geo, n_taps):
    tm = geo.tm
    i = pl.program_id(0)
    tin, tps = geo.tile_in_seq(i)
    is_first = tin == 0
    is_last = tin == tps - 1
    pos0 = tin * tm
    seq_len = tps * tm
    ha = SUBLANES
    hv = BF16_ROWS
    half = n_taps // 2

    za = za_ref[...]
    za_s[0:ha, :] = jnp.where(is_first, 0.0, zap_ref[...])
    za_s[ha:ha + tm, :] = za
    za_s[ha + tm:, :] = jnp.where(is_last, 0.0, zan_ref[...])
    t = pos0 + lax.broadcasted_iota(jnp.int32, (tm, 1), 0)
    n_pool = pw_ref.shape[0]
    pc = pw_ref.shape[1]
    for gi in range(n_pool):
        w = POOL_WINDOWS[gi]
        lo, hi = -(w // 2), w - w // 2
        c0 = gi * pc
        acc = za_s[ha + lo:ha + lo + tm, c0:c0 + pc]
        for o in range(lo + 1, hi):
            acc = acc + za_s[ha + o:ha + o + tm, c0:c0 + pc]
        cnt = (jnp.minimum(t + hi, seq_len) - jnp.maximum(t + lo, 0)).astype(F32)
        dlt = acc / cnt - za[:, c0:c0 + pc]
        y = _dot(dlt.astype(BF), pw_ref[gi]) * ps_ref[:, c0:c0 + pc]
        ma_ref[:, c0:c0 + pc] = y.astype(BF)

    cw = v_ref.shape[1]
    rows = tm + 2 * hv
    vs_s[0, 0:hv, :] = jnp.where(is_first, 0.0, vp_ref[...].astype(F32))
    vs_s[0, hv:hv + tm, :] = v_ref[...].astype(F32)
    vs_s[0, hv + tm:rows, :] = jnp.where(is_last, 0.0, vn_ref[...].astype(F32))
    shifted = rows - SUBLANES
    for r in range(1, SUBLANES):
        vs_s[r, 0:shifted, :] = vs_s[0, r:r + shifted, :]

    base = hv - half

    def conv_rows(rb, carry):
        r0 = pl.multiple_of(rb * CONV_ROWS, CONV_ROWS)
        for cb in range(cw // LANES):
            c0 = cb * LANES
            acc = jnp.broadcast_to(dwb_ref[:, c0:c0 + LANES], (CONV_ROWS, LANES))
            for k in range(n_taps):
                q, r = divmod(k + base, SUBLANES)
                win = vs_s[r, pl.ds(r0 + q * SUBLANES, CONV_ROWS), c0:c0 + LANES]
                acc = acc + win * dw_ref[k:k + 1, c0:c0 + LANES]
            cv_s[pl.ds(r0, CONV_ROWS), c0:c0 + LANES] = acc
        return carry

    lax.fori_loop(0, tm // CONV_ROWS, conv_rows, 0)

    u = cv_s[...]
    uc = u - jnp.mean(u, axis=-1, keepdims=True)
    var = jnp.mean(uc * uc, axis=-1, keepdims=True)
    y = uc * lax.rsqrt(var + EPS) * lng_ref[...] + lnb_ref[...]
    y = y * jax.nn.sigmoid(y)
    mb_ref[...] = (_dot(y.astype(BF), pww_ref[...]) + pwb_ref[...]).astype(BF)


def _mix(za, v, pool_w, pool_scale, dw_w, dw_b, ln_g, ln_b, pw_w, pw_b, geo):
    tm = geo.tm
    n, pw = za.shape
    cw = v.shape[1]
    n_taps = dw_w.shape[0]
    assert n_taps // 2 < BF16_ROWS and max(POOL_WINDOWS) // 2 <= SUBLANES
    assert tm % CONV_ROWS == 0 and tm % BF16_ROWS == 0
    ra, rv = tm // SUBLANES, tm // BF16_ROWS
    full = lambda a: pl.BlockSpec(a.shape, lambda i: (0,) * a.ndim)
    return pl.pallas_call(
        functools.partial(_mix_kernel, geo=geo, n_taps=n_taps),
        grid=(geo.nt,),
        in_specs=[
            pl.BlockSpec((tm, pw), lambda i: (i, 0)),
            pl.BlockSpec((SUBLANES, pw), lambda i: (jnp.maximum(i * ra - 1, 0), 0)),
            pl.BlockSpec((SUBLANES, pw), lambda i: (jnp.minimum((i + 1) * ra, n // SUBLANES - 1), 0)),
            pl.BlockSpec((tm, cw), lambda i: (i, 0)),
            pl.BlockSpec((BF16_ROWS, cw), lambda i: (jnp.maximum(i * rv - 1, 0), 0)),
            pl.BlockSpec((BF16_ROWS, cw), lambda i: (jnp.minimum((i + 1) * rv, n // BF16_ROWS - 1), 0)),
            full(pool_w), full(pool_scale), full(dw_w), full(dw_b), full(ln_g), full(ln_b),
            full(pw_w), full(pw_b),
        ],
        out_specs=[
            pl.BlockSpec((tm, pw), lambda i: (i, 0)),
            pl.BlockSpec((tm, cw), lambda i: (i, 0)),
        ],
        out_shape=[jax.ShapeDtypeStruct((n, pw), BF), jax.ShapeDtypeStruct((n, cw), BF)],
        scratch_shapes=[
            pltpu.VMEM((tm + 2 * SUBLANES, pw), F32),
            pltpu.VMEM((SUBLANES, tm + 2 * BF16_ROWS, cw), F32),
            pltpu.VMEM((tm, cw), F32),
        ],
        compiler_params=_params("arbitrary"),
        name="mix",
    )(za, za, za, v, v, v, pool_w, pool_scale, dw_w, dw_b, ln_g, ln_b, pw_w, pw_b)


def _dft_split(s):
    so = 1 << ((int(math.log2(s)) + 1) // 2)
    so = max(min(so, 128), 8)
    assert s % so == 0
    return so, s // so


def _dft_tables(s, so, si, ch):
    k = np.arange(so)
    ang1 = 2.0 * np.pi * ((k[:, None] * k[None, :]) % so) / so
    f1 = np.concatenate([np.cos(ang1), -np.sin(ang1)], axis=0)
    ka = np.arange(so)[:, None, None]
    kb = np.arange(si)[None, :, None]
    ni = np.arange(si)[None, None, :]
    num = (ni * kb * so + ni * ka) % s
    ang2 = 2.0 * np.pi * num / s
    gr = np.cos(ang2) / np.sqrt(s)
    gi = -np.sin(ang2) / np.sqrt(s)
    g = np.concatenate([np.concatenate([gr, -gi], axis=2),
                        np.concatenate([gi, gr], axis=2)], axis=1)
    c = np.arange(ch)
    angc = 2.0 * np.pi * ((c[:, None] * c[None, :]) % ch) / ch
    eye = np.eye(N_FOURIER)
    cbd = np.kron(eye, np.cos(angc) / np.sqrt(ch))
    sbd = np.kron(eye, np.sin(angc) / np.sqrt(ch))
    as_bf = lambda a: jnp.asarray(a.astype(np.float32)).astype(BF)
    return as_bf(f1), as_bf(g), as_bf(cbd), as_bf(sbd)


def _dft1_kernel(f_ref, z_ref, ar_ref, ai_ref, *, so):
    r = _dot(f_ref[...], z_ref[...])
    ar_ref[...] = r[:so].astype(BF)
    ai_ref[...] = r[so:].astype(BF)


def _dft2_kernel(g_ref, ar_ref, ai_ref, cbd_ref, sbd_ref, fw_ref, o_ref, zr_s, zi_s, *, ka, si, w):
    for j in range(ka):
        a = jnp.concatenate([ar_ref[j * si:(j + 1) * si, :], ai_ref[j * si:(j + 1) * si, :]], axis=0)
        z = _dot(g_ref[j], a)
        zr_s[j * si:(j + 1) * si, :] = z[:si].astype(BF)
        zi_s[j * si:(j + 1) * si, :] = z[si:].astype(BF)
    rz = _dot(zr_s[...], cbd_ref[...]) + _dot(zi_s[...], sbd_ref[...])
    out = _dot(rz.astype(BF), fw_ref[...]).astype(BF)
    for j in range(ka):
        o_ref[:, j * w:(j + 1) * w] = out[j * si:(j + 1) * si, :]


def _fourier(zc, fw, nb, s, row_off):
    n, w = zc.shape
    so, si = _dft_split(s)
    assert row_off % s == 0 and si % BF16_ROWS == 0
    f1, g, cbd, sbd = _dft_tables(s, so, si, w // N_FOURIER)
    blk_off = row_off // s
    cols = si * w
    tn = min(cols, 4096)
    ar, ai = pl.pallas_call(
        functools.partial(_dft1_kernel, so=so),
        grid=(nb, cols // tn),
        in_specs=[
            pl.BlockSpec(f1.shape, lambda b, j: (0, 0)),
            pl.BlockSpec((so, tn), lambda b, j: (blk_off + b, j)),
        ],
        out_specs=[pl.BlockSpec((so, tn), lambda b, j: (b, j))] * 2,
        out_shape=[jax.ShapeDtypeStruct((nb * so, cols), BF)] * 2,
        compiler_params=_params("arbitrary", "arbitrary"),
        name="dft1",
    )(f1, zc.reshape(n // si, cols))
    ka = min(so, max(1, 1024 // si))
    full = lambda a: pl.BlockSpec(a.shape, lambda t, b: (0,) * a.ndim)
    kt = so // ka
    out = pl.pallas_call(
        functools.partial(_dft2_kernel, ka=ka, si=si, w=w),
        grid=(kt, nb),
        in_specs=[
            pl.BlockSpec((ka, 2 * si, 2 * si), lambda t, b: (t, 0, 0)),
            pl.BlockSpec((ka * si, w), lambda t, b: (b * kt + t, 0)),
            pl.BlockSpec((ka * si, w), lambda t, b: (b * kt + t, 0)),
            full(cbd), full(sbd), full(fw),
        ],
        out_specs=pl.BlockSpec((si, ka * w), lambda t, b: (b, t)),
        out_shape=jax.ShapeDtypeStruct((nb * si, so * w), BF),
        scratch_shapes=[pltpu.VMEM((ka * si, w), BF), pltpu.VMEM((ka * si, w), BF)],
        compiler_params=_params("arbitrary", "arbitrary"),
        name="dft2",
    )(g, ar.reshape(nb * s, w), ai.reshape(nb * s, w), cbd, sbd, fw)
    return out.reshape(nb * s, w)


def _outproj_kernel(*refs, geo, pw, cw, split):
    if split:
        x0_ref, x1_ref = refs[:2]
        x = _pick_group(geo, x0_ref, x1_ref)
        refs = refs[2:]
    else:
        x = refs[0][...]
        refs = refs[1:]
    (ma_ref, mb_ref, mc0_ref, mc1_ref, wo_ref, g1_ref, ng_ref, sc_ref, sh_ref,
     wrh_ref, wrl_ref, rb_ref, xo_ref, h_ref, lg_ref) = refs
    mc = _pick_group(geo, mc0_ref, mc1_ref)
    y = _dot(ma_ref[...], wo_ref[0:pw, :])
    y = y + _dot(mb_ref[...], wo_ref[pw:pw + cw, :])
    y = y + _dot(mc, wo_ref[pw + cw:, :])
    x = x + g1_ref[...] * y
    xo_ref[...] = x
    h = _rmsnorm(x, ng_ref[...]) * (1.0 + sc_ref[...]) + sh_ref[...]
    _pack_rows(h, h_ref)
    hh = h.astype(BF)
    hl = (h - hh.astype(F32)).astype(BF)
    lg = _dot(hh, wrh_ref[...]) + _dot(hl, wrh_ref[...]) + _dot(hh, wrl_ref[...])
    lg_ref[...] = lg + rb_ref[...]


def _outproj(xs, ma, mb, mc0, mc1, w_out, mod, ng, wrh, wrl, rb, geo):
    d = w_out.shape[1]
    tm = geo.tm
    pw, cw = ma.shape[1], mb.shape[1]
    fw = mc0.shape[1]
    split = len(xs) == 2
    x_specs = _two_group_specs(geo, d) if split else [pl.BlockSpec((tm, d), lambda i: (i, 0))]
    full = lambda a: pl.BlockSpec(a.shape, lambda i: (0,) * a.ndim)
    return pl.pallas_call(
        functools.partial(_outproj_kernel, geo=geo, pw=pw, cw=cw, split=split),
        grid=(geo.nt,),
        in_specs=x_specs + [
            pl.BlockSpec((tm, pw), lambda i: (i, 0)),
            pl.BlockSpec((tm, cw), lambda i: (i, 0)),
        ] + _two_group_specs(geo, fw) + [
            pl.BlockSpec(w_out.shape, lambda i: (0, 0), pipeline_mode=pl.Buffered(1)),
            _mod_spec(geo, d, 2),
            full(ng),
            _mod_spec(geo, d, 4),
            _mod_spec(geo, d, 3),
            full(wrh), full(wrl), full(rb),
        ],
        out_specs=[
            pl.BlockSpec((tm, d), lambda i: (i, 0)),
            pl.BlockSpec((tm * _slab_rows(d), LANES), lambda i: (i, 0)),
            pl.BlockSpec((tm, LANES), lambda i: (i, 0)),
        ],
        out_shape=[
            jax.ShapeDtypeStruct((geo.n, d), F32),
            jax.ShapeDtypeStruct((geo.n * _slab_rows(d), LANES), jnp.uint32),
            jax.ShapeDtypeStruct((geo.n, LANES), F32),
        ],
        compiler_params=_params("arbitrary"),
        name="outproj",
    )(*xs, ma, mb, mc0, mc1, w_out, mod, ng, mod, mod, wrh, wrl, rb)


def _route_kernel(lg_ref, info_ref, dest_ref, cnt_ref, carry_s, *, n_exp, n_grp, blk):
    p = pl.program_id(0)
    i = pl.program_id(1)
    tm = lg_ref.shape[0]
    epg = n_exp // n_grp

    @pl.when((p == 0) & (i == 0))
    def _():
        carry_s[...] = jnp.zeros_like(carry_s)

    @pl.when((p == 1) & (i == 0))
    def _():
        cnt = carry_s[...]
        cnt_ref[...] = cnt
        nblk = jnp.floor((cnt + float(blk - 1)) * (1.0 / blk))
        hi = jnp.floor(nblk * (1.0 / 256.0))
        lo = nblk - 256.0 * hi
        r = lax.broadcasted_iota(jnp.int32, (LANES, LANES), 0)
        c = lax.broadcasted_iota(jnp.int32, (LANES, LANES), 1)
        upper = jnp.where(r < c, 1.0, 0.0).astype(BF)
        start_blk = 256.0 * _dot(hi.astype(BF), upper) + _dot(lo.astype(BF), upper)
        carry_s[...] = start_blk * float(blk)

    lg = lg_ref[...]
    lane = lax.broadcasted_iota(jnp.int32, (tm, LANES), 1).astype(F32)
    big = float(1 << 20)
    neg = -jnp.inf
    is_c = (lane >= n_exp) & (lane < n_exp + n_grp)
    lc = jnp.where(is_c, lg, neg)
    m = jnp.max(lc, axis=1, keepdims=True)
    gidx = jnp.min(jnp.where(lc == m, lane - n_exp, big), axis=1, keepdims=True)
    se = jnp.sum(jnp.where(is_c, jnp.exp(lc - m), 0.0), axis=1, keepdims=True)
    p_grp = 1.0 / se
    in_g = (lane >= gidx * epg) & (lane < (gidx + 1.0) * epg)
    lf = jnp.where(in_g, lg, neg)
    v1 = jnp.max(lf, axis=1, keepdims=True)
    i1 = jnp.min(jnp.where((lf == v1) & in_g, lane, big), axis=1, keepdims=True)
    in_g2 = in_g & (lane != i1)
    lf2 = jnp.where(in_g2, lg, neg)
    v2 = jnp.max(lf2, axis=1, keepdims=True)
    i2 = jnp.min(jnp.where((lf2 == v2) & in_g2, lane, big), axis=1, keepdims=True)
    e2 = jnp.exp(v2 - v1)
    den = 1.0 + e2
    w1 = p_grp * (1.0 / den)
    w2 = p_grp * (e2 / den)

    sel1 = lane == i1
    sel2 = lane == i2
    oh = jnp.where(sel1 | sel2, 1.0, 0.0)
    rr = lax.broadcasted_iota(jnp.int32, (tm, tm), 0)
    cc = lax.broadcasted_iota(jnp.int32, (tm, tm), 1)
    lower = jnp.where(cc < rr, 1.0, 0.0).astype(BF)
    slot = carry_s[0:1, :] + _dot(lower, oh.astype(BF))
    d1 = jnp.sum(jnp.where(sel1, slot, 0.0), axis=1, keepdims=True)
    d2 = jnp.sum(jnp.where(sel2, slot, 0.0), axis=1, keepdims=True)
    carry_s[...] = carry_s[...] + jnp.sum(oh, axis=0, keepdims=True)

    @pl.when(p == 1)
    def _():
        info_ref[...] = jnp.where(lane == 0, w1, jnp.where(lane == 1, w2, 0.0))
        dd = jnp.where(lane == 0, d1, jnp.where(lane == 1, d2, 0.0))
        dest_ref[...] = jnp.transpose(dd)[0:SUBLANES, :].astype(jnp.int32)


def _route(lg, n_exp, n_grp, blk, tm):
    n = lg.shape[0]
    nt = n // tm
    return pl.pallas_call(
        functools.partial(_route_kernel, n_exp=n_exp, n_grp=n_grp, blk=blk),
        grid=(2, nt),
        in_specs=[pl.BlockSpec((tm, LANES), lambda p, i: (i, 0))],
        out_specs=[
            pl.BlockSpec((tm, LANES), lambda p, i: (i * p, 0)),
            pl.BlockSpec((None, SUBLANES, tm), lambda p, i: (i * p, 0, 0)),
            pl.BlockSpec((SUBLANES, LANES), lambda p, i: (0, 0)),
        ],
        out_shape=[
            jax.ShapeDtypeStruct((n, LANES), F32),
            jax.ShapeDtypeStruct((nt, SUBLANES, tm), jnp.int32),
            jax.ShapeDtypeStruct((SUBLANES, LANES), F32),
        ],
        scratch_shapes=[pltpu.VMEM((SUBLANES, LANES), F32)],
        compiler_params=_params("arbitrary", "arbitrary"),
        name="route",
    )(lg)


def _dispatch_kernel(cnt_ref, pst_ref, nu_ref, dest_hbm, h_ref, xs_hbm, dsm, isem, sem, zblk, *, blk, n_exp, ns):
    i = pl.program_id(0)
    tm = h_ref.shape[0] // ns
    cp = pltpu.make_async_copy(dest_hbm.at[i], dsm, isem)
    cp.start()
    cp.wait()

    def row_copy(t, k):
        return pltpu.make_async_copy(_slab_at(h_ref, t, ns), _slab_at(xs_hbm, dsm[k, t], ns), sem)

    def issue(tb, c):
        for u in range(DMA_UNROLL):
            for k in range(TOP_K):
                row_copy(tb * DMA_UNROLL + u, k).start()
        return c

    lax.fori_loop(0, tm // DMA_UNROLL, issue, 0)

    def drain(tb, c):
        for u in range(DMA_UNROLL):
            for k in range(TOP_K):
                row_copy(tb * DMA_UNROLL + u, k).wait()
        return c

    lax.fori_loop(0, tm // DMA_UNROLL, drain, 0)

    @pl.when(i == pl.num_programs(0) - 1)
    def _():
        zblk[...] = jnp.zeros_like(zblk)

        def per_expert(e, c):
            cnt = cnt_ref[e]
            first = pst_ref[e] + cnt
            n_pad = (blk - cnt % blk) % blk

            def pad_row(r, c2):
                z = pltpu.make_async_copy(_slab_at(zblk, 0, ns), _slab_at(xs_hbm, first + r, ns), sem)
                z.start()
                z.wait()
                return c2

            lax.fori_loop(0, n_pad, pad_row, 0)
            return c

        lax.fori_loop(0, n_exp, per_expert, 0)

        def unused_block(b, c):
            z = pltpu.make_async_copy(zblk, _slab_at(xs_hbm, b, blk * ns), sem)
            z.start()
            z.wait()
            return c

        lax.fori_loop(nu_ref[0], xs_hbm.shape[0] // (blk * ns), unused_block, 0)


def _dispatch(cnt, pstart, n_used, dest, h, n_rows, ns, blk, tm):
    n = h.shape[0] // ns
    n_exp = cnt.shape[0]
    return pl.pallas_call(
        functools.partial(_dispatch_kernel, blk=blk, n_exp=n_exp, ns=ns),
        grid_spec=pltpu.PrefetchScalarGridSpec(
            num_scalar_prefetch=3,
            grid=(n // tm,),
            in_specs=[
                pl.BlockSpec(memory_space=pl.ANY),
                pl.BlockSpec((tm * ns, LANES), lambda i, c, p, u: (i, 0)),
            ],
            out_specs=pl.BlockSpec(memory_space=pl.ANY),
            scratch_shapes=[
                pltpu.SMEM((SUBLANES, tm), jnp.int32),
                pltpu.SemaphoreType.DMA,
                pltpu.SemaphoreType.DMA,
                pltpu.VMEM((blk * ns, LANES), h.dtype),
            ],
        ),
        out_shape=jax.ShapeDtypeStruct((n_rows * ns, LANES), h.dtype),
        compiler_params=_params("arbitrary"),
        name="dispatch",
    )(cnt, pstart, n_used, dest, h)


def _expert_kernel(be_ref, bi_ref, nu_ref, xs_ref, w1_ref, w3_ref, w2_ref, ys_ref, w1_s, w3_s, w2_s, *, blk):
    b = pl.program_id(0)
    used = b < nu_ref[0]

    @pl.when(jnp.logical_not(used))
    def _():
        ys_ref[...] = jnp.zeros_like(ys_ref)

    @pl.when(used & ((b == 0) | (be_ref[b] != be_ref[jnp.maximum(b - 1, 0)])))
    def _():
        w1_s[...] = w1_ref[...].astype(BF)
        w3_s[...] = w3_ref[...].astype(BF)
        w2_s[...] = w2_ref[...].astype(BF)

    @pl.when(used)
    def _():
        x = _unpack_rows(xs_ref, blk, BF)
        a = _dot(x, w1_s[...])
        c = _dot(x, w3_s[...])
        hm = (a * jax.nn.sigmoid(a) * c).astype(BF)
        _pack_rows(_dot(hm, w2_s[...]), ys_ref)


def _experts(blk_e, blk_i, n_used, xs, w1, w3, w2, layer, blk):
    d, de = w1.shape[2:]
    ns = _slab_rows(d)
    w_idx = lambda b, be, bi, nu: (layer, be[b], 0, 0)
    return pl.pallas_call(
        functools.partial(_expert_kernel, blk=blk),
        grid_spec=pltpu.PrefetchScalarGridSpec(
            num_scalar_prefetch=3,
            grid=(xs.shape[0] // (blk * ns),),
            in_specs=[
                pl.BlockSpec((blk * ns, LANES), lambda b, be, bi, nu: (bi[b], 0)),
                pl.BlockSpec((None, None, d, de), w_idx),
                pl.BlockSpec((None, None, d, de), w_idx),
                pl.BlockSpec((None, None, de, d), w_idx),
            ],
            out_specs=pl.BlockSpec((blk * ns, LANES), lambda b, be, bi, nu: (b, 0)),
            scratch_shapes=[pltpu.VMEM((d, de), BF), pltpu.VMEM((d, de), BF), pltpu.VMEM((de, d), BF)],
        ),
        out_shape=jax.ShapeDtypeStruct(xs.shape, xs.dtype),
        compiler_params=_params("arbitrary"),
        name="experts",
    )(blk_e, blk_i, n_used, xs, w1, w3, w2)


def _combine_kernel(*refs, geo, final):
    if final:
        dest_hbm, ys_hbm, x_ref, info_ref, g2_ref, fg_ref, o0_ref, o1_ref, dsm, isem, buf0, buf1, sem = refs
    else:
        dest_hbm, ys_hbm, x_ref, info_ref, g2_ref, o_ref, dsm, isem, buf0, buf1, sem = refs
    bufs = (buf0, buf1)
    i = pl.program_id(0)
    tm, d = x_ref.shape
    ns = _slab_rows(d)
    cp = pltpu.make_async_copy(dest_hbm.at[i], dsm, isem)
    cp.start()
    cp.wait()

    def row_copy(t, k):
        return pltpu.make_async_copy(_slab_at(ys_hbm, dsm[k, t], ns), _slab_at(bufs[k], t, ns), sem)

    def issue(tb, c):
        for u in range(DMA_UNROLL):
            for k in range(TOP_K):
                row_copy(tb * DMA_UNROLL + u, k).start()
        return c

    lax.fori_loop(0, tm // DMA_UNROLL, issue, 0)

    def drain(tb, c):
        for u in range(DMA_UNROLL):
            for k in range(TOP_K):
                row_copy(tb * DMA_UNROLL + u, k).wait()
        return c

    lax.fori_loop(0, tm // DMA_UNROLL, drain, 0)

    y = _unpack_rows(buf0, tm, F32) * info_ref[:, 0:1] + _unpack_rows(buf1, tm, F32) * info_ref[:, 1:2]
    x = x_ref[...] + g2_ref[...] * y
    if final:
        x = _rmsnorm(x, fg_ref[...])

        @pl.when(i < geo.n0t)
        def _():
            o0_ref[...] = x

        @pl.when(i >= geo.n0t)
        def _():
            o1_ref[...] = x
    else:
        o_ref[...] = x


def _combine(dest, ys, x, info, mod, geo, final_g=None):
    n, d = x.shape
    tm = geo.tm
    final = final_g is not None
    in_specs = [
        pl.BlockSpec(memory_space=pl.ANY),
        pl.BlockSpec(memory_space=pl.ANY),
        pl.BlockSpec((tm, d), lambda i: (i, 0)),
        pl.BlockSpec((tm, LANES), lambda i: (i, 0)),
        _mod_spec(geo, d, 5),
    ]
    args = [dest, ys, x, info, mod]
    if final:
        in_specs.append(pl.BlockSpec((1, d), lambda i: (0, 0)))
        args.append(final_g)
        out_specs = _two_group_specs(geo, d)
        out_shape = [jax.ShapeDtypeStruct((geo.n0, d), F32), jax.ShapeDtypeStruct((geo.n1, d), F32)]
    else:
        out_specs = pl.BlockSpec((tm, d), lambda i: (i, 0))
        out_shape = jax.ShapeDtypeStruct((n, d), F32)
    return pl.pallas_call(
        functools.partial(_combine_kernel, geo=geo, final=final),
        grid=(geo.nt,),
        in_specs=in_specs,
        out_specs=out_specs,
        out_shape=out_shape,
        scratch_shapes=[
            pltpu.SMEM((SUBLANES, tm), jnp.int32),
            pltpu.SemaphoreType.DMA,
            pltpu.VMEM((tm * _slab_rows(d), LANES), ys.dtype),
            pltpu.VMEM((tm * _slab_rows(d), LANES), ys.dtype),
            pltpu.SemaphoreType.DMA,
        ],
        compiler_params=_params("arbitrary"),
        name="combine",
    )(*args)


def _block_tables(cnt_f, n_exp, blk, n_blocks):
    cnt = cnt_f[0, :n_exp].astype(jnp.int32)
    nblk = (cnt + (blk - 1)) // blk
    end_blk = jnp.cumsum(nblk)
    n_used = end_blk[-1]
    pstart = (end_blk - nblk) * blk
    b = jnp.minimum(jnp.arange(n_blocks, dtype=jnp.int32), n_used - 1)
    blk_e = jnp.minimum(jnp.sum(end_blk[None, :] <= b[:, None], axis=1), n_exp - 1).astype(jnp.int32)
    return cnt, pstart.astype(jnp.int32), blk_e, b, n_used.reshape(1).astype(jnp.int32)


def kernel(x_prompt, x_sample, c_prompt, c_sample, w_ada, b_ada, norm1_g, w_in, pool_w, pool_scale, conv_dw_w, conv_dw_b, conv_ln_g, conv_ln_b, conv_pw_w, conv_pw_b, fourier_w, w_out, norm2_g, router_coarse_w, router_coarse_b, router_fine_w, router_fine_b, expert_w1, expert_w3, expert_w2, final_g):
    b0, s0, d = x_prompt.shape
    b1, s1, _ = x_sample.shape
    depth = w_ada.shape[0]
    pw = pool_w.shape[1] * pool_w.shape[2]
    cw = conv_pw_w.shape[1]
    fw = fourier_w.shape[1]
    n_grp = router_coarse_w.shape[2]
    epg = router_fine_w.shape[3]
    n_exp = n_grp * epg
    assert n_exp + n_grp <= LANES and pool_w.shape[1] == len(POOL_WINDOWS) and d % (2 * LANES) == 0
    tm = math.gcd(math.gcd(s0, s1), 256)
    geo = _Geom(b0, s0, b1, s1, tm)
    n = geo.n
    n_seq = b0 + b1
    blk = ROUTE_ROWS
    n_blocks = -(-(n * TOP_K + n_exp * (blk - 1)) // blk)

    c = jnp.concatenate([c_prompt, c_sample], axis=0)
    mod = _ada(c, w_ada, b_ada).reshape(depth, n_seq, 6, 1, d)

    xs = [x_prompt.reshape(geo.n0, d), x_sample.reshape(geo.n1, d)]
    for l in range(depth):
        ml = mod[l]
        za, v, zc = _inproj(xs, norm1_g[l].reshape(1, d), ml, w_in[l].astype(BF), geo, pw, cw, fw)
        ma, mb = _mix(za, v, pool_w[l].astype(BF), pool_scale[l].reshape(1, pw), conv_dw_w[l],
                      conv_dw_b[l].reshape(1, cw), conv_ln_g[l].reshape(1, cw), conv_ln_b[l].reshape(1, cw),
                      conv_pw_w[l].astype(BF), conv_pw_b[l].reshape(1, cw), geo)
        fwl = fourier_w[l].astype(BF)
        mc0 = _fourier(zc, fwl, b0, s0, 0)
        mc1 = _fourier(zc, fwl, b1, s1, geo.n0)

        wr = jnp.concatenate([
            jnp.transpose(router_fine_w[l], (1, 0, 2)).reshape(d, n_exp),
            router_coarse_w[l],
            jnp.zeros((d, LANES - n_exp - n_grp), F32)], axis=1)
        wrh = wr.astype(BF)
        wrl = (wr - wrh.astype(F32)).astype(BF)
        rb = jnp.concatenate([router_fine_b[l].reshape(n_exp), router_coarse_b[l],
                              jnp.zeros((LANES - n_exp - n_grp,), F32)]).reshape(1, LANES)
        x, h, lg = _outproj(xs, ma, mb, mc0, mc1, w_out[l].astype(BF), ml, norm2_g[l].reshape(1, d),
                            wrh, wrl, rb, geo)

        info, dest, cnt_f = _route(lg, n_exp, n_grp, blk, tm)
        cnt, pstart, blk_e, blk_i, n_used = _block_tables(cnt_f, n_exp, blk, n_blocks)
        xsort = _dispatch(cnt, pstart, n_used, dest, h, n_blocks * blk, _slab_rows(d), blk, tm)
        ys = _experts(blk_e, blk_i, n_used, xsort, expert_w1, expert_w3, expert_w2, l, blk)
        if l + 1 < depth:
            xs = [_combine(dest, ys, x, info, ml, geo)]
        else:
            y0, y1 = _combine(dest, ys, x, info, ml, geo, final_g.reshape(1, d))
    return (y0.reshape(b0, s0, d), y1.reshape(b1, s1, d))
```

```python
import functools
import math

import numpy as np
import jax
import jax.numpy as jnp
from jax import lax
from jax.experimental import pallas as pl
from jax.experimental.pallas import tpu as pltpu

EPS = 1e-6
POOL_WINDOWS = (2, 4, 8, 16)
N_FOURIER = 4
TOP_K = 2
BF = jnp.bfloat16
F32 = jnp.float32

LANES = 128
SUBLANES = 8
BF16_ROWS = 16
VMEM_LIMIT = 56 * 1024 * 1024
ROUTE_ROWS = 256
CONV_ROWS = 64
DMA_UNROLL = 8
MM_ROWS = 512
MM_CHUNK = 256
ROUTE_TILES = 4


def _dot(a, b):
    return jnp.dot(a, b, preferred_element_type=F32)


def _params(*sem):
    return pltpu.CompilerParams(dimension_semantics=sem, vmem_limit_bytes=VMEM_LIMIT)


class _Geom:
    def __init__(self, b0, s0, b1, s1, tm):
        assert s0 % tm == 0 and s1 % tm == 0
        self.b0, self.s0, self.b1, self.s1, self.tm = b0, s0, b1, s1, tm
        self.n0, self.n1 = b0 * s0, b1 * s1
        self.n = self.n0 + self.n1
        self.n0t, self.nt = self.n0 // tm, self.n // tm
        self.tps0, self.tps1 = s0 // tm, s1 // tm

    def seq_of_tile(self, i):
        return jnp.where(i < self.n0t, i // self.tps0, self.b0 + (i - self.n0t) // self.tps1)

    def tile_in_seq(self, i):
        tin = jnp.where(i < self.n0t, i % self.tps0, (i - self.n0t) % self.tps1)
        tps = jnp.where(i < self.n0t, self.tps0, self.tps1)
        return tin, tps


def _rmsnorm(x, g):
    return x * lax.rsqrt(jnp.mean(x * x, axis=-1, keepdims=True) + EPS) * g


HI_MASK = 0xFFFF0000


def _slab_rows(d):
    return d // (2 * LANES)


def _pack_rows(x, slab_ref, row0=0):
    rows, d = x.shape
    half, ns = d // 2, _slab_rows(d)
    bits = pltpu.bitcast(x.astype(BF).astype(F32), jnp.uint32)
    words = (bits[:, :half] >> 16) | (bits[:, half:] & jnp.uint32(HI_MASK))
    for s in range(ns):
        slab_ref[pl.ds(row0 * ns + s, rows, stride=ns), :] = words[:, s * LANES:(s + 1) * LANES]


def _unpack_rows(slab_ref, rows, dtype):
    ns = slab_ref.shape[0] // rows
    lo, hi = [], []
    for s in range(ns):
        words = slab_ref[pl.ds(s, rows, stride=ns), :]
        lo.append(pltpu.bitcast(words << 16, F32).astype(dtype))
        hi.append(pltpu.bitcast(words & jnp.uint32(HI_MASK), F32).astype(dtype))
    return jnp.concatenate(lo + hi, axis=1)


def _slab_at(ref, row, ns):
    return ref.at[pl.ds(pl.multiple_of(row * ns, ns), ns), :]


def _ada_kernel(c_ref, w_ref, b_ref, o_ref):
    c = c_ref[...]
    s = (c * jax.nn.sigmoid(c)).astype(BF)
    o_ref[...] = _dot(s, w_ref[...].astype(BF)) + b_ref[...]


def _ada(c, w_ada, b_ada):
    depth, d, d6 = w_ada.shape
    nb = c.shape[0]
    tn = min(1024, d6)
    return pl.pallas_call(
        _ada_kernel,
        grid=(depth, d6 // tn),
        in_specs=[
            pl.BlockSpec((nb, d), lambda l, j: (0, 0)),
            pl.BlockSpec((None, d, tn), lambda l, j: (l, 0, j)),
            pl.BlockSpec((None, 1, tn), lambda l, j: (l, 0, j)),
        ],
        out_specs=pl.BlockSpec((None, nb, tn), lambda l, j: (l, 0, j)),
        out_shape=jax.ShapeDtypeStruct((depth, nb, d6), F32),
        compiler_params=_params("arbitrary", "arbitrary"),
        name="ada",
    )(c, w_ada, b_ada.reshape(depth, 1, d6))


def _mod_spec(geo, d, j):
    return pl.BlockSpec((None, None, 1, d), lambda i: (geo.seq_of_tile(i), j, 0, 0))


def _two_group_specs(geo, width):
    tm = geo.tm
    return [
        pl.BlockSpec((tm, width), lambda i: (jnp.minimum(i, geo.n0t - 1), 0)),
        pl.BlockSpec((tm, width), lambda i: (jnp.maximum(i - geo.n0t, 0), 0)),
    ]


def _pick_group(geo, ref0, ref1, rows=slice(None)):
    return jnp.where(pl.program_id(0) < geo.n0t, ref0[rows, :], ref1[rows, :])


def _row_chunks(tm):
    cr = min(tm, MM_CHUNK)
    return [slice(r, r + cr) for r in range(0, tm, cr)]


def _inproj_kernel(*refs, geo, pw, cw, split):
    if split:
        x0_ref, x1_ref, g_ref, sc_ref, sh_ref, w_ref, za_ref, v_ref, zc_ref = refs
    else:
        x_ref, g_ref, sc_ref, sh_ref, w_ref, za_ref, v_ref, zc_ref = refs
    for rs in _row_chunks(geo.tm):
        x = _pick_group(geo, x0_ref, x1_ref, rs) if split else x_ref[rs, :]
        h = _rmsnorm(x, g_ref[...]) * (1.0 + sc_ref[...]) + sh_ref[...]
        hb = h.astype(BF)
        za_ref[rs, :] = _dot(hb, w_ref[:, :pw])
        a = _dot(hb, w_ref[:, pw:pw + cw])
        b = _dot(hb, w_ref[:, pw + cw:pw + 2 * cw])
        v_ref[rs, :] = (a * jax.nn.sigmoid(b)).astype(BF)
        zc_ref[rs, :] = _dot(hb, w_ref[:, pw + 2 * cw:]).astype(BF)


def _inproj(xs, g, mod, w_in, geo, pw, cw, fw):
    d = w_in.shape[0]
    tm = geo.tm
    split = len(xs) == 2
    x_specs = _two_group_specs(geo, d) if split else [pl.BlockSpec((tm, d), lambda i: (i, 0))]
    return pl.pallas_call(
        functools.partial(_inproj_kernel, geo=geo, pw=pw, cw=cw, split=split),
        grid=(geo.nt,),
        in_specs=x_specs + [
            pl.BlockSpec((1, d), lambda i: (0, 0)),
            _mod_spec(geo, d, 1),
            _mod_spec(geo, d, 0),
            pl.BlockSpec(w_in.shape, lambda i: (0, 0), pipeline_mode=pl.Buffered(1)),
        ],
        out_specs=[
            pl.BlockSpec((tm, pw), lambda i: (i, 0)),
            pl.BlockSpec((tm, cw), lambda i: (i, 0)),
            pl.BlockSpec((tm, fw), lambda i: (i, 0)),
        ],
        out_shape=[
            jax.ShapeDtypeStruct((geo.n, pw), F32),
            jax.ShapeDtypeStruct((geo.n, cw), BF),
            jax.ShapeDtypeStruct((geo.n, fw), BF),
        ],
        compiler_params=_params("arbitrary"),
        name="inproj",
    )(*xs, g, mod, mod, w_in)


def _mix_kernel(za_ref, zap_ref, zan_ref, v_ref, vp_ref, vn_ref, pw_ref, ps_ref,
                dw_ref, dwb_ref, lng_ref, lnb_ref, pww_ref, pwb_ref,
                ma_ref, mb_ref, za_s, vs_s, cv_s, *, geo, n_taps):
    tm = geo.tm
    i = pl.program_id(0)
    tin, tps = geo.tile_in_seq(i)
    is_first = tin == 0
    is_last = tin == tps - 1
    pos0 = tin * tm
    seq_len = tps * tm
    ha = SUBLANES
    hv = BF16_ROWS
    half = n_taps // 2

    za = za_ref[...]
    za_s[0:ha, :] = jnp.where(is_first, 0.0, zap_ref[...])
    za_s[ha:ha + tm, :] = za
    za_s[ha + tm:, :] = jnp.where(is_last, 0.0, zan_ref[...])
    t = pos0 + lax.broadcasted_iota(jnp.int32, (tm, 1), 0)
    n_pool = pw_ref.shape[0]
    pc = pw_ref.shape[1]
    for gi in range(n_pool):
        w = POOL_WINDOWS[gi]
        lo, hi = -(w // 2), w - w // 2
        c0 = gi * pc
        acc = za_s[ha + lo:ha + lo + tm, c0:c0 + pc]
        for o in range(lo + 1, hi):
            acc = acc + za_s[ha + o:ha + o + tm, c0:c0 + pc]
        cnt = (jnp.minimum(t + hi, seq_len) - jnp.maximum(t + lo, 0)).astype(F32)
        dlt = acc / cnt - za[:, c0:c0 + pc]
        y = _dot(dlt.astype(BF), pw_ref[gi]) * ps_ref[:, c0:c0 + pc]
        ma_ref[:, c0:c0 + pc] = y.astype(BF)

    cw = v_ref.shape[1]
    rows = tm + 2 * hv
    vs_s[0, 0:hv, :] = jnp.where(is_first, 0.0, vp_ref[...].astype(F32))
    vs_s[0, hv:hv + tm, :] = v_ref[...].astype(F32)
    vs_s[0, hv + tm:rows, :] = jnp.where(is_last, 0.0, vn_ref[...].astype(F32))
    shifted = rows - SUBLANES
    for r in range(1, SUBLANES):
        vs_s[r, 0:shifted, :] = vs_s[0, r:r + shifted, :]

    base = hv - half

    def conv_rows(rb, carry):
        r0 = pl.multiple_of(rb * CONV_ROWS, CONV_ROWS)
        for cb in range(cw // LANES):
            c0 = cb * LANES
            acc = jnp.broadcast_to(dwb_ref[:, c0:c0 + LANES], (CONV_ROWS, LANES))
            for r in range(SUBLANES):
                qs = [q for q in range(-(-(n_taps + base) // SUBLANES)) if 0 <= q * SUBLANES + r - base < n_taps]
                if not qs:
                    continue
                span = CONV_ROWS + (qs[-1] - qs[0]) * SUBLANES
                rows = vs_s[r, pl.ds(r0 + qs[0] * SUBLANES, span), c0:c0 + LANES]
                for q in qs:
                    k = q * SUBLANES + r - base
                    off = (q - qs[0]) * SUBLANES
                    acc = acc + rows[off:off + CONV_ROWS, :] * dw_ref[k:k + 1, c0:c0 + LANES]
            cv_s[pl.ds(r0, CONV_ROWS), c0:c0 + LANES] = acc
        return carry

    lax.fori_loop(0, tm // CONV_ROWS, conv_rows, 0)

    u = cv_s[...]
    uc = u - jnp.mean(u, axis=-1, keepdims=True)
    var = jnp.mean(uc * uc, axis=-1, keepdims=True)
    y = uc * lax.rsqrt(var + EPS) * lng_ref[...] + lnb_ref[...]
    y = y * jax.nn.sigmoid(y)
    mb_ref[...] = (_dot(y.astype(BF), pww_ref[...]) + pwb_ref[...]).astype(BF)


def _mix(za, v, pool_w, pool_scale, dw_w, dw_b, ln_g, ln_b, pw_w, pw_b, geo):
    tm = geo.tm
    n, pw = za.shape
    cw = v.shape[1]
    n_taps = dw_w.shape[0]
    assert n_taps // 2 < BF16_ROWS and max(POOL_WINDOWS) // 2 <= SUBLANES
    assert tm % CONV_ROWS == 0 and tm % BF16_ROWS == 0
    ra, rv = tm // SUBLANES, tm // BF16_ROWS
    full = lambda a: pl.BlockSpec(a.shape, lambda i: (0,) * a.ndim)
    return pl.pallas_call(
        functools.partial(_mix_kernel, geo=geo, n_taps=n_taps),
        grid=(geo.nt,),
        in_specs=[
            pl.BlockSpec((tm, pw), lambda i: (i, 0)),
            pl.BlockSpec((SUBLANES, pw), lambda i: (jnp.maximum(i * ra - 1, 0), 0)),
            pl.BlockSpec((SUBLANES, pw), lambda i: (jnp.minimum((i + 1) * ra, n // SUBLANES - 1), 0)),
            pl.BlockSpec((tm, cw), lambda i: (i, 0)),
            pl.BlockSpec((BF16_ROWS, cw), lambda i: (jnp.maximum(i * rv - 1, 0), 0)),
            pl.BlockSpec((BF16_ROWS, cw), lambda i: (jnp.minimum((i + 1) * rv, n // BF16_ROWS - 1), 0)),
            full(pool_w), full(pool_scale), full(dw_w), full(dw_b), full(ln_g), full(ln_b),
            full(pw_w), full(pw_b),
        ],
        out_specs=[
            pl.BlockSpec((tm, pw), lambda i: (i, 0)),
            pl.BlockSpec((tm, cw), lambda i: (i, 0)),
        ],
        out_shape=[jax.ShapeDtypeStruct((n, pw), BF), jax.ShapeDtypeStruct((n, cw), BF)],
        scratch_shapes=[
            pltpu.VMEM((tm + 2 * SUBLANES, pw), F32),
            pltpu.VMEM((SUBLANES, tm + 2 * BF16_ROWS, cw), F32),
            pltpu.VMEM((tm, cw), F32),
        ],
        compiler_params=_params("arbitrary"),
        name="mix",
    )(za, za, za, v, v, v, pool_w, pool_scale, dw_w, dw_b, ln_g, ln_b, pw_w, pw_b)


def _dft_split(s):
    so = 1 << ((int(math.log2(s)) + 1) // 2)
    so = max(min(so, 128), 8)
    assert s % so == 0
    return so, s // so


def _dft_tables(s, so, si, ch):
    k = np.arange(so)
    ang1 = 2.0 * np.pi * ((k[:, None] * k[None, :]) % so) / so
    f1 = np.concatenate([np.cos(ang1), -np.sin(ang1)], axis=0)
    ka = np.arange(so)[:, None, None]
    kb = np.arange(si)[None, :, None]
    ni = np.arange(si)[None, None, :]
    num = (ni * kb * so + ni * ka) % s
    ang2 = 2.0 * np.pi * num / s
    gr = np.cos(ang2) / np.sqrt(s)
    gi = -np.sin(ang2) / np.sqrt(s)
    g = np.concatenate([np.concatenate([gr, -gi], axis=2),
                        np.concatenate([gi, gr], axis=2)], axis=1)
    c = np.arange(ch)
    angc = 2.0 * np.pi * ((c[:, None] * c[None, :]) % ch) / ch
    eye = np.eye(N_FOURIER)
    cbd = np.kron(eye, np.cos(angc) / np.sqrt(ch))
    sbd = np.kron(eye, np.sin(angc) / np.sqrt(ch))
    as_bf = lambda a: jnp.asarray(a.astype(np.float32)).astype(BF)
    return as_bf(f1), as_bf(g), as_bf(cbd), as_bf(sbd)


def _dft1_kernel(f_ref, z_ref, ar_ref, ai_ref, *, so):
    r = _dot(f_ref[...], z_ref[...])
    ar_ref[...] = r[:so].astype(BF)
    ai_ref[...] = r[so:].astype(BF)


def _dft2_kernel(g_ref, ar_ref, ai_ref, cbd_ref, sbd_ref, fw_ref, o_ref, zr_s, zi_s, *, ka, si, w):
    for j in range(ka):
        a = jnp.concatenate([ar_ref[j * si:(j + 1) * si, :], ai_ref[j * si:(j + 1) * si, :]], axis=0)
        z = _dot(g_ref[j], a)
        zr_s[j * si:(j + 1) * si, :] = z[:si].astype(BF)
        zi_s[j * si:(j + 1) * si, :] = z[si:].astype(BF)
    rz = _dot(zr_s[...], cbd_ref[...]) + _dot(zi_s[...], sbd_ref[...])
    out = _dot(rz.astype(BF), fw_ref[...]).astype(BF)
    for j in range(ka):
        o_ref[:, j * w:(j + 1) * w] = out[j * si:(j + 1) * si, :]


def _fourier(zc, fw, nb, s, row_off):
    n, w = zc.shape
    so, si = _dft_split(s)
    assert row_off % s == 0 and si % BF16_ROWS == 0
    f1, g, cbd, sbd = _dft_tables(s, so, si, w // N_FOURIER)
    blk_off = row_off // s
    cols = si * w
    tn = min(cols, 4096)
    ar, ai = pl.pallas_call(
        functools.partial(_dft1_kernel, so=so),
        grid=(nb, cols // tn),
        in_specs=[
            pl.BlockSpec(f1.shape, lambda b, j: (0, 0)),
            pl.BlockSpec((so, tn), lambda b, j: (blk_off + b, j)),
        ],
        out_specs=[pl.BlockSpec((so, tn), lambda b, j: (b, j))] * 2,
        out_shape=[jax.ShapeDtypeStruct((nb * so, cols), BF)] * 2,
        compiler_params=_params("arbitrary", "arbitrary"),
        name="dft1",
    )(f1, zc.reshape(n // si, cols))
    ka = min(so, max(1, 1024 // si))
    full = lambda a: pl.BlockSpec(a.shape, lambda t, b: (0,) * a.ndim)
    kt = so // ka
    out = pl.pallas_call(
        functools.partial(_dft2_kernel, ka=ka, si=si, w=w),
        grid=(kt, nb),
        in_specs=[
            pl.BlockSpec((ka, 2 * si, 2 * si), lambda t, b: (t, 0, 0)),
            pl.BlockSpec((ka * si, w), lambda t, b: (b * kt + t, 0)),
            pl.BlockSpec((ka * si, w), lambda t, b: (b * kt + t, 0)),
            full(cbd), full(sbd), full(fw),
        ],
        out_specs=pl.BlockSpec((si, ka * w), lambda t, b: (b, t)),
        out_shape=jax.ShapeDtypeStruct((nb * si, so * w), BF),
        scratch_shapes=[pltpu.VMEM((ka * si, w), BF), pltpu.VMEM((ka * si, w), BF)],
        compiler_params=_params("arbitrary", "arbitrary"),
        name="dft2",
    )(g, ar.reshape(nb * s, w), ai.reshape(nb * s, w), cbd, sbd, fw)
    return out.reshape(nb * s, w)


def _outproj_kernel(*refs, geo, pw, cw, split):
    x_refs, refs = (refs[:2], refs[2:]) if split else (refs[:1], refs[1:])
    (ma_ref, mb_ref, mc0_ref, mc1_ref, wo_ref, g1_ref, ng_ref, sc_ref, sh_ref,
     wr_ref, rb_ref, xo_ref, h_ref, lg_ref) = refs
    for rs in _row_chunks(geo.tm):
        x = _pick_group(geo, x_refs[0], x_refs[1], rs) if split else x_refs[0][rs, :]
        mc = _pick_group(geo, mc0_ref, mc1_ref, rs)
        y = _dot(ma_ref[rs, :], wo_ref[0:pw, :])
        y = y + _dot(mb_ref[rs, :], wo_ref[pw:pw + cw, :])
        y = y + _dot(mc, wo_ref[pw + cw:, :])
        x = x + g1_ref[...] * y
        xo_ref[rs, :] = x
        h = _rmsnorm(x, ng_ref[...]) * (1.0 + sc_ref[...]) + sh_ref[...]
        _pack_rows(h, h_ref, rs.start)
        lg_ref[rs, :] = _dot(h.astype(BF), wr_ref[...]) + rb_ref[...]


def _outproj(xs, ma, mb, mc0, mc1, w_out, mod, ng, wr, rb, geo):
    d = w_out.shape[1]
    tm = geo.tm
    pw, cw = ma.shape[1], mb.shape[1]
    fw = mc0.shape[1]
    split = len(xs) == 2
    x_specs = _two_group_specs(geo, d) if split else [pl.BlockSpec((tm, d), lambda i: (i, 0))]
    full = lambda a: pl.BlockSpec(a.shape, lambda i: (0,) * a.ndim)
    return pl.pallas_call(
        functools.partial(_outproj_kernel, geo=geo, pw=pw, cw=cw, split=split),
        grid=(geo.nt,),
        in_specs=x_specs + [
            pl.BlockSpec((tm, pw), lambda i: (i, 0)),
            pl.BlockSpec((tm, cw), lambda i: (i, 0)),
        ] + _two_group_specs(geo, fw) + [
            pl.BlockSpec(w_out.shape, lambda i: (0, 0), pipeline_mode=pl.Buffered(1)),
            _mod_spec(geo, d, 2),
            full(ng),
            _mod_spec(geo, d, 4),
            _mod_spec(geo, d, 3),
            full(wr), full(rb),
        ],
        out_specs=[
            pl.BlockSpec((tm, d), lambda i: (i, 0)),
            pl.BlockSpec((tm * _slab_rows(d), LANES), lambda i: (i, 0)),
            pl.BlockSpec((tm, LANES), lambda i: (i, 0)),
        ],
        out_shape=[
            jax.ShapeDtypeStruct((geo.n, d), F32),
            jax.ShapeDtypeStruct((geo.n * _slab_rows(d), LANES), jnp.uint32),
            jax.ShapeDtypeStruct((geo.n, LANES), F32),
        ],
        compiler_params=_params("arbitrary"),
        name="outproj",
    )(*xs, ma, mb, mc0, mc1, w_out, mod, ng, mod, mod, wr, rb)


def _route_kernel(lg_ref, info_ref, dest_ref, cnt_ref, carry_s, lower_s, *, n_exp, n_grp, blk):
    p = pl.program_id(0)
    i = pl.program_id(1)
    tm = lg_ref.shape[0]
    epg = n_exp // n_grp

    @pl.when((p == 0) & (i == 0))
    def _():
        carry_s[...] = jnp.zeros_like(carry_s)
        rr = lax.broadcasted_iota(jnp.int32, (tm, tm), 0)
        cc = lax.broadcasted_iota(jnp.int32, (tm, tm), 1)
        lower_s[...] = jnp.where(cc < rr, 1.0, 0.0).astype(BF)

    @pl.when((p == 1) & (i == 0))
    def _():
        cnt = carry_s[...]
        cnt_ref[...] = cnt
        nblk = jnp.floor((cnt + float(blk - 1)) * (1.0 / blk))
        hi = jnp.floor(nblk * (1.0 / 256.0))
        lo = nblk - 256.0 * hi
        r = lax.broadcasted_iota(jnp.int32, (LANES, LANES), 0)
        c = lax.broadcasted_iota(jnp.int32, (LANES, LANES), 1)
        upper = jnp.where(r < c, 1.0, 0.0).astype(BF)
        start_blk = 256.0 * _dot(hi.astype(BF), upper) + _dot(lo.astype(BF), upper)
        carry_s[...] = start_blk * float(blk)

    lg = lg_ref[...]
    lane = lax.broadcasted_iota(jnp.int32, (tm, LANES), 1).astype(F32)
    big = float(1 << 20)
    neg = -jnp.inf
    is_c = (lane >= n_exp) & (lane < n_exp + n_grp)
    lc = jnp.where(is_c, lg, neg)
    m = jnp.max(lc, axis=1, keepdims=True)
    gidx = jnp.min(jnp.where(lc == m, lane - n_exp, big), axis=1, keepdims=True)
    se = jnp.sum(jnp.where(is_c, jnp.exp(lc - m), 0.0), axis=1, keepdims=True)
    p_grp = 1.0 / se
    in_g = (lane >= gidx * epg) & (lane < (gidx + 1.0) * epg)
    lf = jnp.where(in_g, lg, neg)
    v1 = jnp.max(lf, axis=1, keepdims=True)
    i1 = jnp.min(jnp.where((lf == v1) & in_g, lane, big), axis=1, keepdims=True)
    in_g2 = in_g & (lane != i1)
    lf2 = jnp.where(in_g2, lg, neg)
    v2 = jnp.max(lf2, axis=1, keepdims=True)
    i2 = jnp.min(jnp.where((lf2 == v2) & in_g2, lane, big), axis=1, keepdims=True)
    e2 = jnp.exp(v2 - v1)
    den = 1.0 + e2
    w1 = p_grp * (1.0 / den)
    w2 = p_grp * (e2 / den)

    sel1 = lane == i1
    sel2 = lane == i2
    oh = jnp.where(sel1 | sel2, 1.0, 0.0)
    slot = carry_s[0:1, :] + _dot(lower_s[...], oh.astype(BF))
    d1 = jnp.sum(jnp.where(sel1, slot, 0.0), axis=1, keepdims=True)
    d2 = jnp.sum(jnp.where(sel2, slot, 0.0), axis=1, keepdims=True)
    carry_s[...] = carry_s[...] + jnp.sum(oh, axis=0, keepdims=True)

    @pl.when(p == 1)
    def _():
        info_ref[...] = jnp.where(lane == 0, w1, jnp.where(lane == 1, w2, 0.0))
        dd = jnp.where(lane == 0, d1, jnp.where(lane == 1, d2, 0.0))
        dt = jnp.transpose(dd)[0:SUBLANES, :].astype(jnp.int32)
        tt = dest_ref.shape[2]
        for j in range(dest_ref.shape[0]):
            dest_ref[j] = dt[:, j * tt:(j + 1) * tt]


def _route(lg, n_exp, n_grp, blk, tm):
    n = lg.shape[0]
    nt = n // tm
    per = math.gcd(nt, ROUTE_TILES)
    tr = per * tm
    assert blk & (blk - 1) == 0
    return pl.pallas_call(
        functools.partial(_route_kernel, n_exp=n_exp, n_grp=n_grp, blk=blk),
        grid=(2, n // tr),
        in_specs=[pl.BlockSpec((tr, LANES), lambda p, i: (i, 0))],
        out_specs=[
            pl.BlockSpec((tr, LANES), lambda p, i: (i * p, 0)),
            pl.BlockSpec((per, SUBLANES, tm), lambda p, i: (i * p, 0, 0)),
            pl.BlockSpec((SUBLANES, LANES), lambda p, i: (0, 0)),
        ],
        out_shape=[
            jax.ShapeDtypeStruct((n, LANES), F32),
            jax.ShapeDtypeStruct((nt, SUBLANES, tm), jnp.int32),
            jax.ShapeDtypeStruct((SUBLANES, LANES), F32),
        ],
        scratch_shapes=[pltpu.VMEM((SUBLANES, LANES), F32), pltpu.VMEM((tr, tr), BF)],
        compiler_params=_params("arbitrary", "arbitrary"),
        name="route",
    )(lg)


def _dispatch_kernel(cnt_ref, pst_ref, nu_ref, dest_hbm, h_ref, xs_hbm, dsm, isem, sem, zblk, *, blk, n_exp, ns):
    i = pl.program_id(0)
    tm = h_ref.shape[0] // ns
    cp = pltpu.make_async_copy(dest_hbm.at[i], dsm, isem)
    cp.start()
    cp.wait()

    def row_copy(t, k):
        return pltpu.make_async_copy(_slab_at(h_ref, t, ns), _slab_at(xs_hbm, dsm[k, t], ns), sem)

    def issue(tb, c):
        for u in range(DMA_UNROLL):
            for k in range(TOP_K):
                row_copy(tb * DMA_UNROLL + u, k).start()
        return c

    lax.fori_loop(0, tm // DMA_UNROLL, issue, 0)

    def drain(tb, c):
        for u in range(DMA_UNROLL):
            for k in range(TOP_K):
                row_copy(tb * DMA_UNROLL + u, k).wait()
        return c

    lax.fori_loop(0, tm // DMA_UNROLL, drain, 0)

    @pl.when(i == pl.num_programs(0) - 1)
    def _():
        zblk[...] = jnp.zeros_like(zblk)

        def per_expert(e, c):
            cnt = cnt_ref[e]
            first = pst_ref[e] + cnt
            n_pad = (blk - cnt % blk) % blk

            def pad_copy(r):
                return pltpu.make_async_copy(_slab_at(zblk, 0, ns), _slab_at(xs_hbm, first + r, ns), sem)

            def pad_start(r, c2):
                pad_copy(r).start()
                return c2

            def pad_wait(r, c2):
                pad_copy(r).wait()
                return c2

            lax.fori_loop(0, n_pad, pad_start, 0)
            lax.fori_loop(0, n_pad, pad_wait, 0)
            return c

        lax.fori_loop(0, n_exp, per_expert, 0)

        def unused_block(b, c):
            z = pltpu.make_async_copy(zblk, _slab_at(xs_hbm, b, blk * ns), sem)
            z.start()
            z.wait()
            return c

        lax.fori_loop(nu_ref[0], xs_hbm.shape[0] // (blk * ns), unused_block, 0)


def _dispatch(cnt, pstart, n_used, dest, h, n_rows, ns, blk, tm):
    n = h.shape[0] // ns
    n_exp = cnt.shape[0]
    return pl.pallas_call(
        functools.partial(_dispatch_kernel, blk=blk, n_exp=n_exp, ns=ns),
        grid_spec=pltpu.PrefetchScalarGridSpec(
            num_scalar_prefetch=3,
            grid=(n // tm,),
            in_specs=[
                pl.BlockSpec(memory_space=pl.ANY),
                pl.BlockSpec((tm * ns, LANES), lambda i, c, p, u: (i, 0)),
            ],
            out_specs=pl.BlockSpec(memory_space=pl.ANY),
            scratch_shapes=[
                pltpu.SMEM((SUBLANES, tm), jnp.int32),
                pltpu.SemaphoreType.DMA,
                pltpu.SemaphoreType.DMA,
                pltpu.VMEM((blk * ns, LANES), h.dtype),
            ],
        ),
        out_shape=jax.ShapeDtypeStruct((n_rows * ns, LANES), h.dtype),
        compiler_params=_params("arbitrary"),
        name="dispatch",
    )(cnt, pstart, n_used, dest, h)


def _expert_kernel(be_ref, bi_ref, nu_ref, xs_ref, w1_ref, w3_ref, w2_ref, ys_ref, w1_s, w3_s, w2_s, *, blk):
    b = pl.program_id(0)
    used = b < nu_ref[0]

    @pl.when(jnp.logical_not(used))
    def _():
        ys_ref[...] = jnp.zeros_like(ys_ref)

    @pl.when(used & ((b == 0) | (be_ref[b] != be_ref[jnp.maximum(b - 1, 0)])))
    def _():
        w1_s[...] = w1_ref[...].astype(BF)
        w3_s[...] = w3_ref[...].astype(BF)
        w2_s[...] = w2_ref[...].astype(BF)

    @pl.when(used)
    def _():
        x = _unpack_rows(xs_ref, blk, BF)
        a = _dot(x, w1_s[...])
        c = _dot(x, w3_s[...])
        hm = (a * jax.nn.sigmoid(a) * c).astype(BF)
        _pack_rows(_dot(hm, w2_s[...]), ys_ref)


def _experts(blk_e, blk_i, n_used, xs, w1, w3, w2, layer, blk):
    d, de = w1.shape[2:]
    ns = _slab_rows(d)
    w_idx = lambda b, be, bi, nu: (layer, be[b], 0, 0)
    return pl.pallas_call(
        functools.partial(_expert_kernel, blk=blk),
        grid_spec=pltpu.PrefetchScalarGridSpec(
            num_scalar_prefetch=3,
            grid=(xs.shape[0] // (blk * ns),),
            in_specs=[
                pl.BlockSpec((blk * ns, LANES), lambda b, be, bi, nu: (bi[b], 0)),
                pl.BlockSpec((None, None, d, de), w_idx),
                pl.BlockSpec((None, None, d, de), w_idx),
                pl.BlockSpec((None, None, de, d), w_idx),
            ],
            out_specs=pl.BlockSpec((blk * ns, LANES), lambda b, be, bi, nu: (b, 0)),
            scratch_shapes=[pltpu.VMEM((d, de), BF), pltpu.VMEM((d, de), BF), pltpu.VMEM((de, d), BF)],
        ),
        out_shape=jax.ShapeDtypeStruct(xs.shape, xs.dtype),
        compiler_params=_params("arbitrary"),
        name="experts",
    )(blk_e, blk_i, n_used, xs, w1, w3, w2)


def _combine_kernel(*refs, geo, final):
    if final:
        dest_hbm, ys_hbm, x_ref, info_ref, g2_ref, fg_ref, o0_ref, o1_ref = refs[:8]
    else:
        dest_hbm, ys_hbm, x_ref, info_ref, g2_ref, o_ref = refs[:6]
    dsm, isem, sems, b00, b01, b10, b11 = refs[-7:]
    bufs = ((b00, b01), (b10, b11))
    i = pl.program_id(0)
    nt = pl.num_programs(0)
    tm, d = x_ref.shape
    ns = _slab_rows(d)

    def row_copy(slot, t, k):
        return pltpu.make_async_copy(_slab_at(ys_hbm, dsm[slot, k, t], ns), _slab_at(bufs[slot][k], t, ns),
                                     sems.at[slot])

    def start_gather(tile, slot):
        cp = pltpu.make_async_copy(dest_hbm.at[tile], dsm.at[slot], isem)
        cp.start()
        cp.wait()

        def issue(tb, c):
            for u in range(DMA_UNROLL):
                for k in range(TOP_K):
                    row_copy(slot, tb * DMA_UNROLL + u, k).start()
            return c

        lax.fori_loop(0, tm // DMA_UNROLL, issue, 0)

    def finish_gather(slot):
        def drain(tb, c):
            for u in range(DMA_UNROLL):
                for k in range(TOP_K):
                    row_copy(slot, tb * DMA_UNROLL + u, k).wait()
            return c

        lax.fori_loop(0, tm // DMA_UNROLL, drain, 0)

    @pl.when(i == 0)
    def _():
        start_gather(0, 0)

    def step(slot):
        @pl.when(i + 1 < nt)
        def _():
            start_gather(i + 1, 1 - slot)

        finish_gather(slot)
        y = (_unpack_rows(bufs[slot][0], tm, F32) * info_ref[:, 0:1]
             + _unpack_rows(bufs[slot][1], tm, F32) * info_ref[:, 1:2])
        x = x_ref[...] + g2_ref[...] * y
        if final:
            x = _rmsnorm(x, fg_ref[...])

            @pl.when(i < geo.n0t)
            def _():
                o0_ref[...] = x

            @pl.when(i >= geo.n0t)
            def _():
                o1_ref[...] = x
        else:
            o_ref[...] = x

    for slot in range(2):
        pl.when(i % 2 == slot)(functools.partial(step, slot))


def _combine(dest, ys, x, info, mod, geo, final_g=None):
    n, d = x.shape
    tm = geo.tm
    final = final_g is not None
    in_specs = [
        pl.BlockSpec(memory_space=pl.ANY),
        pl.BlockSpec(memory_space=pl.ANY),
        pl.BlockSpec((tm, d), lambda i: (i, 0)),
        pl.BlockSpec((tm, LANES), lambda i: (i, 0)),
        _mod_spec(geo, d, 5),
    ]
    args = [dest, ys, x, info, mod]
    if final:
        in_specs.append(pl.BlockSpec((1, d), lambda i: (0, 0)))
        args.append(final_g)
        out_specs = _two_group_specs(geo, d)
        out_shape = [jax.ShapeDtypeStruct((geo.n0, d), F32), jax.ShapeDtypeStruct((geo.n1, d), F32)]
    else:
        out_specs = pl.BlockSpec((tm, d), lambda i: (i, 0))
        out_shape = jax.ShapeDtypeStruct((n, d), F32)
    return pl.pallas_call(
        functools.partial(_combine_kernel, geo=geo, final=final),
        grid=(geo.nt,),
        in_specs=in_specs,
        out_specs=out_specs,
        out_shape=out_shape,
        scratch_shapes=[
            pltpu.SMEM((2, SUBLANES, tm), jnp.int32),
            pltpu.SemaphoreType.DMA,
            pltpu.SemaphoreType.DMA((2,)),
        ] + [pltpu.VMEM((tm * _slab_rows(d), LANES), ys.dtype)] * (2 * TOP_K),
        compiler_params=_params("arbitrary"),
        name="combine",
    )(*args)


def _block_tables(cnt_f, n_exp, blk, n_blocks):
    cnt = cnt_f[0, :n_exp].astype(jnp.int32)
    nblk = (cnt + (blk - 1)) // blk
    end_blk = jnp.cumsum(nblk)
    n_used = end_blk[-1]
    pstart = (end_blk - nblk) * blk
    b = jnp.minimum(jnp.arange(n_blocks, dtype=jnp.int32), n_used - 1)
    blk_e = jnp.minimum(jnp.sum(end_blk[None, :] <= b[:, None], axis=1), n_exp - 1).astype(jnp.int32)
    return cnt, pstart.astype(jnp.int32), blk_e, b, n_used.reshape(1).astype(jnp.int32)


def kernel(x_prompt, x_sample, c_prompt, c_sample, w_ada, b_ada, norm1_g, w_in, pool_w, pool_scale, conv_dw_w, conv_dw_b, conv_ln_g, conv_ln_b, conv_pw_w, conv_pw_b, fourier_w, w_out, norm2_g, router_coarse_w, router_coarse_b, router_fine_w, router_fine_b, expert_w1, expert_w3, expert_w2, final_g):
    b0, s0, d = x_prompt.shape
    b1, s1, _ = x_sample.shape
    depth = w_ada.shape[0]
    pw = pool_w.shape[1] * pool_w.shape[2]
    cw = conv_pw_w.shape[1]
    fw = fourier_w.shape[1]
    n_grp = router_coarse_w.shape[2]
    epg = router_fine_w.shape[3]
    n_exp = n_grp * epg
    assert n_exp + n_grp <= LANES and pool_w.shape[1] == len(POOL_WINDOWS) and d % (2 * LANES) == 0
    tm = math.gcd(math.gcd(s0, s1), 256)
    geo = _Geom(b0, s0, b1, s1, tm)
    geo_mm = _Geom(b0, s0, b1, s1, math.gcd(math.gcd(s0, s1), MM_ROWS))
    n = geo.n
    n_seq = b0 + b1
    blk = ROUTE_ROWS
    n_blocks = -(-(n * TOP_K + n_exp * (blk - 1)) // blk)

    c = jnp.concatenate([c_prompt, c_sample], axis=0)
    mod = _ada(c, w_ada, b_ada).reshape(depth, n_seq, 6, 1, d)

    xs = [x_prompt.reshape(geo.n0, d), x_sample.reshape(geo.n1, d)]
    for l in range(depth):
        ml = mod[l]
        za, v, zc = _inproj(xs, norm1_g[l].reshape(1, d), ml, w_in[l].astype(BF), geo_mm, pw, cw, fw)
        ma, mb = _mix(za, v, pool_w[l].astype(BF), pool_scale[l].reshape(1, pw), conv_dw_w[l],
                      conv_dw_b[l].reshape(1, cw), conv_ln_g[l].reshape(1, cw), conv_ln_b[l].reshape(1, cw),
                      conv_pw_w[l].astype(BF), conv_pw_b[l].reshape(1, cw), geo)
        fwl = fourier_w[l].astype(BF)
        mc0 = _fourier(zc, fwl, b0, s0, 0)
        mc1 = _fourier(zc, fwl, b1, s1, geo.n0)

        wr = jnp.concatenate([
            jnp.transpose(router_fine_w[l], (1, 0, 2)).reshape(d, n_exp),
            router_coarse_w[l],
            jnp.zeros((d, LANES - n_exp - n_grp), F32)], axis=1)
        rb = jnp.concatenate([router_fine_b[l].reshape(n_exp), router_coarse_b[l],
                              jnp.zeros((LANES - n_exp - n_grp,), F32)]).reshape(1, LANES)
        x, h, lg = _outproj(xs, ma, mb, mc0, mc1, w_out[l].astype(BF), ml, norm2_g[l].reshape(1, d),
                            wr.astype(BF), rb, geo_mm)

        info, dest, cnt_f = _route(lg, n_exp, n_grp, blk, tm)
        cnt, pstart, blk_e, blk_i, n_used = _block_tables(cnt_f, n_exp, blk, n_blocks)
        xsort = _dispatch(cnt, pstart, n_used, dest, h, n_blocks * blk, _slab_rows(d), blk, tm)
        ys = _experts(blk_e, blk_i, n_used, xsort, expert_w1, expert_w3, expert_w2, l, blk)
        if l + 1 < depth:
            xs = [_combine(dest, ys, x, info, ml, geo)]
        else:
            y0, y1 = _combine(dest, ys, x, info, ml, geo, final_g.reshape(1, d))
    return (y0.reshape(b0, s0, d), y1.reshape(b1, s1, d))
```

```python
import functools
import math

import numpy as np
import jax
import jax.numpy as jnp
from jax import lax
from jax.experimental import pallas as pl
from jax.experimental.pallas import tpu as pltpu

EPS = 1e-6
POOL_WINDOWS = (2, 4, 8, 16)
N_FOURIER = 4
TOP_K = 2
BF = jnp.bfloat16
F32 = jnp.float32

LANES = 128
SUBLANES = 8
BF16_ROWS = 16
VMEM_LIMIT = 56 * 1024 * 1024
ROUTE_ROWS = 256
CONV_ROWS = 64
DMA_UNROLL = 8
MM_ROWS = 512
MM_CHUNK = 256
ROUTE_TILES = 4


def _dot(a, b):
    return jnp.dot(a, b, preferred_element_type=F32)


def _params(*sem):
    return pltpu.CompilerParams(dimension_semantics=sem, vmem_limit_bytes=VMEM_LIMIT)


class _Geom:
    def __init__(self, b0, s0, b1, s1, tm):
        assert s0 % tm == 0 and s1 % tm == 0
        self.b0, self.s0, self.b1, self.s1, self.tm = b0, s0, b1, s1, tm
        self.n0, self.n1 = b0 * s0, b1 * s1
        self.n = self.n0 + self.n1
        self.n0t, self.nt = self.n0 // tm, self.n // tm
        self.tps0, self.tps1 = s0 // tm, s1 // tm

    def seq_of_tile(self, i):
        return jnp.where(i < self.n0t, i // self.tps0, self.b0 + (i - self.n0t) // self.tps1)

    def tile_in_seq(self, i):
        tin = jnp.where(i < self.n0t, i % self.tps0, (i - self.n0t) % self.tps1)
        tps = jnp.where(i < self.n0t, self.tps0, self.tps1)
        return tin, tps


def _rmsnorm(x, g):
    return x * lax.rsqrt(jnp.mean(x * x, axis=-1, keepdims=True) + EPS) * g


HI_MASK = 0xFFFF0000


def _slab_rows(d):
    return d // (2 * LANES)


def _pack_rows(x, slab_ref, row0=0):
    rows, d = x.shape
    half, ns = d // 2, _slab_rows(d)
    bits = pltpu.bitcast(x.astype(BF).astype(F32), jnp.uint32)
    words = (bits[:, :half] >> 16) | (bits[:, half:] & jnp.uint32(HI_MASK))
    for s in range(ns):
        slab_ref[pl.ds(row0 * ns + s, rows, stride=ns), :] = words[:, s * LANES:(s + 1) * LANES]


def _pack_cols(lo, hi, slab_ref, s0):
    rows = lo.shape[0]
    ns = slab_ref.shape[0] // rows
    lo_bits = pltpu.bitcast(lo.astype(BF).astype(F32), jnp.uint32)
    hi_bits = pltpu.bitcast(hi.astype(BF).astype(F32), jnp.uint32)
    words = (lo_bits >> 16) | (hi_bits & jnp.uint32(HI_MASK))
    for j in range(lo.shape[1] // LANES):
        slab_ref[pl.ds(s0 + j, rows, stride=ns), :] = words[:, j * LANES:(j + 1) * LANES]


def _unpack_rows(slab_ref, rows, dtype):
    ns = slab_ref.shape[0] // rows
    lo, hi = [], []
    for s in range(ns):
        words = slab_ref[pl.ds(s, rows, stride=ns), :]
        lo.append(pltpu.bitcast(words << 16, F32).astype(dtype))
        hi.append(pltpu.bitcast(words & jnp.uint32(HI_MASK), F32).astype(dtype))
    return jnp.concatenate(lo + hi, axis=1)


def _slab_at(ref, row, ns):
    return ref.at[pl.ds(pl.multiple_of(row * ns, ns), ns), :]


def _ada_kernel(c_ref, w_ref, b_ref, o_ref):
    c = c_ref[...]
    s = (c * jax.nn.sigmoid(c)).astype(BF)
    o_ref[...] = _dot(s, w_ref[...].astype(BF)) + b_ref[...]


def _ada(c, w_ada, b_ada):
    depth, d, d6 = w_ada.shape
    nb = c.shape[0]
    tn = min(1024, d6)
    return pl.pallas_call(
        _ada_kernel,
        grid=(depth, d6 // tn),
        in_specs=[
            pl.BlockSpec((nb, d), lambda l, j: (0, 0)),
            pl.BlockSpec((None, d, tn), lambda l, j: (l, 0, j)),
            pl.BlockSpec((None, 1, tn), lambda l, j: (l, 0, j)),
        ],
        out_specs=pl.BlockSpec((None, nb, tn), lambda l, j: (l, 0, j)),
        out_shape=jax.ShapeDtypeStruct((depth, nb, d6), F32),
        compiler_params=_params("arbitrary", "arbitrary"),
        name="ada",
    )(c, w_ada, b_ada.reshape(depth, 1, d6))


def _mod_spec(geo, d, j):
    return pl.BlockSpec((None, None, 1, d), lambda i: (geo.seq_of_tile(i), j, 0, 0))


def _two_group_specs(geo, width):
    tm = geo.tm
    return [
        pl.BlockSpec((tm, width), lambda i: (jnp.minimum(i, geo.n0t - 1), 0)),
        pl.BlockSpec((tm, width), lambda i: (jnp.maximum(i - geo.n0t, 0), 0)),
    ]


def _pick_group(geo, ref0, ref1, rows=slice(None)):
    return jnp.where(pl.program_id(0) < geo.n0t, ref0[rows, :], ref1[rows, :])


def _row_chunks(tm):
    cr = min(tm, MM_CHUNK)
    return [slice(r, r + cr) for r in range(0, tm, cr)]


def _inproj_kernel(*refs, geo, pw, cw, split):
    if split:
        x0_ref, x1_ref, g_ref, sc_ref, sh_ref, w_ref, za_ref, v_ref, zc_ref = refs
    else:
        x_ref, g_ref, sc_ref, sh_ref, w_ref, za_ref, v_ref, zc_ref = refs
    for rs in _row_chunks(geo.tm):
        x = _pick_group(geo, x0_ref, x1_ref, rs) if split else x_ref[rs, :]
        h = _rmsnorm(x, g_ref[...]) * (1.0 + sc_ref[...]) + sh_ref[...]
        hb = h.astype(BF)
        za_ref[rs, :] = _dot(hb, w_ref[:, :pw])
        a = _dot(hb, w_ref[:, pw:pw + cw])
        b = _dot(hb, w_ref[:, pw + cw:pw + 2 * cw])
        v_ref[rs, :] = (a * jax.nn.sigmoid(b)).astype(BF)
        zc_ref[rs, :] = _dot(hb, w_ref[:, pw + 2 * cw:]).astype(BF)


def _inproj(xs, g, mod, w_in, geo, pw, cw, fw):
    d = w_in.shape[0]
    tm = geo.tm
    split = len(xs) == 2
    x_specs = _two_group_specs(geo, d) if split else [pl.BlockSpec((tm, d), lambda i: (i, 0))]
    return pl.pallas_call(
        functools.partial(_inproj_kernel, geo=geo, pw=pw, cw=cw, split=split),
        grid=(geo.nt,),
        in_specs=x_specs + [
            pl.BlockSpec((1, d), lambda i: (0, 0)),
            _mod_spec(geo, d, 1),
            _mod_spec(geo, d, 0),
            pl.BlockSpec(w_in.shape, lambda i: (0, 0), pipeline_mode=pl.Buffered(1)),
        ],
        out_specs=[
            pl.BlockSpec((tm, pw), lambda i: (i, 0)),
            pl.BlockSpec((tm, cw), lambda i: (i, 0)),
            pl.BlockSpec((tm, fw), lambda i: (i, 0)),
        ],
        out_shape=[
            jax.ShapeDtypeStruct((geo.n, pw), F32),
            jax.ShapeDtypeStruct((geo.n, cw), BF),
            jax.ShapeDtypeStruct((geo.n, fw), BF),
        ],
        compiler_params=_params("arbitrary"),
        name="inproj",
    )(*xs, g, mod, mod, w_in)


def _mix_kernel(za_ref, zap_ref, zan_ref, v_ref, vp_ref, vn_ref, pw_ref, ps_ref,
                dw_ref, dwb_ref, lng_ref, lnb_ref, pww_ref, pwb_ref,
                ma_ref, mb_ref, za_s, vs_s, cv_s, *, geo, n_taps):
    tm = geo.tm
    i = pl.program_id(0)
    tin, tps = geo.tile_in_seq(i)
    is_first = tin == 0
    is_last = tin == tps - 1
    pos0 = tin * tm
    seq_len = tps * tm
    ha = SUBLANES
    hv = BF16_ROWS
    half = n_taps // 2

    za = za_ref[...]
    za_s[0:ha, :] = jnp.where(is_first, 0.0, zap_ref[...])
    za_s[ha:ha + tm, :] = za
    za_s[ha + tm:, :] = jnp.where(is_last, 0.0, zan_ref[...])
    t = pos0 + lax.broadcasted_iota(jnp.int32, (tm, 1), 0)
    n_pool = pw_ref.shape[0]
    pc = pw_ref.shape[1]
    for gi in range(n_pool):
        w = POOL_WINDOWS[gi]
        lo, hi = -(w // 2), w - w // 2
        c0 = gi * pc
        acc = za_s[ha + lo:ha + lo + tm, c0:c0 + pc]
        for o in range(lo + 1, hi):
            acc = acc + za_s[ha + o:ha + o + tm, c0:c0 + pc]
        cnt = (jnp.minimum(t + hi, seq_len) - jnp.maximum(t + lo, 0)).astype(F32)
        dlt = acc / cnt - za[:, c0:c0 + pc]
        y = _dot(dlt.astype(BF), pw_ref[gi]) * ps_ref[:, c0:c0 + pc]
        ma_ref[:, c0:c0 + pc] = y.astype(BF)

    cw = v_ref.shape[1]
    rows = tm + 2 * hv
    vs_s[0, 0:hv, :] = jnp.where(is_first, 0.0, vp_ref[...].astype(F32))
    vs_s[0, hv:hv + tm, :] = v_ref[...].astype(F32)
    vs_s[0, hv + tm:rows, :] = jnp.where(is_last, 0.0, vn_ref[...].astype(F32))
    shifted = rows - SUBLANES
    for r in range(1, SUBLANES):
        vs_s[r, 0:shifted, :] = vs_s[0, r:r + shifted, :]

    base = hv - half

    def conv_rows(rb, carry):
        r0 = pl.multiple_of(rb * CONV_ROWS, CONV_ROWS)
        for cb in range(cw // LANES):
            c0 = cb * LANES
            acc = jnp.broadcast_to(dwb_ref[:, c0:c0 + LANES], (CONV_ROWS, LANES))
            for r in range(SUBLANES):
                qs = [q for q in range(-(-(n_taps + base) // SUBLANES)) if 0 <= q * SUBLANES + r - base < n_taps]
                if not qs:
                    continue
                span = CONV_ROWS + (qs[-1] - qs[0]) * SUBLANES
                rows = vs_s[r, pl.ds(r0 + qs[0] * SUBLANES, span), c0:c0 + LANES]
                for q in qs:
                    k = q * SUBLANES + r - base
                    off = (q - qs[0]) * SUBLANES
                    acc = acc + rows[off:off + CONV_ROWS, :] * dw_ref[k:k + 1, c0:c0 + LANES]
            cv_s[pl.ds(r0, CONV_ROWS), c0:c0 + LANES] = acc
        return carry

    lax.fori_loop(0, tm // CONV_ROWS, conv_rows, 0)

    u = cv_s[...]
    uc = u - jnp.mean(u, axis=-1, keepdims=True)
    var = jnp.mean(uc * uc, axis=-1, keepdims=True)
    y = uc * lax.rsqrt(var + EPS) * lng_ref[...] + lnb_ref[...]
    y = y * jax.nn.sigmoid(y)
    mb_ref[...] = (_dot(y.astype(BF), pww_ref[...]) + pwb_ref[...]).astype(BF)


def _mix(za, v, pool_w, pool_scale, dw_w, dw_b, ln_g, ln_b, pw_w, pw_b, geo):
    tm = geo.tm
    n, pw = za.shape
    cw = v.shape[1]
    n_taps = dw_w.shape[0]
    assert n_taps // 2 < BF16_ROWS and max(POOL_WINDOWS) // 2 <= SUBLANES
    assert tm % CONV_ROWS == 0 and tm % BF16_ROWS == 0
    ra, rv = tm // SUBLANES, tm // BF16_ROWS
    full = lambda a: pl.BlockSpec(a.shape, lambda i: (0,) * a.ndim)
    return pl.pallas_call(
        functools.partial(_mix_kernel, geo=geo, n_taps=n_taps),
        grid=(geo.nt,),
        in_specs=[
            pl.BlockSpec((tm, pw), lambda i: (i, 0)),
            pl.BlockSpec((SUBLANES, pw), lambda i: (jnp.maximum(i * ra - 1, 0), 0)),
            pl.BlockSpec((SUBLANES, pw), lambda i: (jnp.minimum((i + 1) * ra, n // SUBLANES - 1), 0)),
            pl.BlockSpec((tm, cw), lambda i: (i, 0)),
            pl.BlockSpec((BF16_ROWS, cw), lambda i: (jnp.maximum(i * rv - 1, 0), 0)),
            pl.BlockSpec((BF16_ROWS, cw), lambda i: (jnp.minimum((i + 1) * rv, n // BF16_ROWS - 1), 0)),
            full(pool_w), full(pool_scale), full(dw_w), full(dw_b), full(ln_g), full(ln_b),
            full(pw_w), full(pw_b),
        ],
        out_specs=[
            pl.BlockSpec((tm, pw), lambda i: (i, 0)),
            pl.BlockSpec((tm, cw), lambda i: (i, 0)),
        ],
        out_shape=[jax.ShapeDtypeStruct((n, pw), BF), jax.ShapeDtypeStruct((n, cw), BF)],
        scratch_shapes=[
            pltpu.VMEM((tm + 2 * SUBLANES, pw), F32),
            pltpu.VMEM((SUBLANES, tm + 2 * BF16_ROWS, cw), F32),
            pltpu.VMEM((tm, cw), F32),
        ],
        compiler_params=_params("arbitrary"),
        name="mix",
    )(za, za, za, v, v, v, pool_w, pool_scale, dw_w, dw_b, ln_g, ln_b, pw_w, pw_b)


def _dft_split(s):
    so = 1 << ((int(math.log2(s)) + 1) // 2)
    so = max(min(so, 128), 8)
    assert s % so == 0
    return so, s // so


def _dft_tables(s, so, si, ch):
    k = np.arange(so)
    ang1 = 2.0 * np.pi * ((k[:, None] * k[None, :]) % so) / so
    f1 = np.concatenate([np.cos(ang1), -np.sin(ang1)], axis=0)
    ka = np.arange(so)[:, None, None]
    kb = np.arange(si)[None, :, None]
    ni = np.arange(si)[None, None, :]
    num = (ni * kb * so + ni * ka) % s
    ang2 = 2.0 * np.pi * num / s
    gr = np.cos(ang2) / np.sqrt(s)
    gi = -np.sin(ang2) / np.sqrt(s)
    g = np.concatenate([np.concatenate([gr, -gi], axis=2),
                        np.concatenate([gi, gr], axis=2)], axis=1)
    c = np.arange(ch)
    angc = 2.0 * np.pi * ((c[:, None] * c[None, :]) % ch) / ch
    eye = np.eye(N_FOURIER)
    cbd = np.kron(eye, np.cos(angc) / np.sqrt(ch))
    sbd = np.kron(eye, np.sin(angc) / np.sqrt(ch))
    as_bf = lambda a: jnp.asarray(a.astype(np.float32)).astype(BF)
    return as_bf(f1), as_bf(g), as_bf(cbd), as_bf(sbd)


def _dft1_kernel(f_ref, z_ref, ar_ref, ai_ref, *, so):
    r = _dot(f_ref[...], z_ref[...])
    ar_ref[...] = r[:so].astype(BF)
    ai_ref[...] = r[so:].astype(BF)


def _dft2_kernel(g_ref, ar_ref, ai_ref, cbd_ref, sbd_ref, fw_ref, o_ref, zr_s, zi_s, *, ka, si, w):
    for j in range(ka):
        a = jnp.concatenate([ar_ref[j * si:(j + 1) * si, :], ai_ref[j * si:(j + 1) * si, :]], axis=0)
        z = _dot(g_ref[j], a)
        zr_s[j * si:(j + 1) * si, :] = z[:si].astype(BF)
        zi_s[j * si:(j + 1) * si, :] = z[si:].astype(BF)
    rz = _dot(zr_s[...], cbd_ref[...]) + _dot(zi_s[...], sbd_ref[...])
    out = _dot(rz.astype(BF), fw_ref[...]).astype(BF)
    for j in range(ka):
        o_ref[:, j * w:(j + 1) * w] = out[j * si:(j + 1) * si, :]


def _fourier(zc, fw, nb, s, row_off):
    n, w = zc.shape
    so, si = _dft_split(s)
    assert row_off % s == 0 and si % BF16_ROWS == 0
    f1, g, cbd, sbd = _dft_tables(s, so, si, w // N_FOURIER)
    blk_off = row_off // s
    cols = si * w
    tn = min(cols, 4096)
    ar, ai = pl.pallas_call(
        functools.partial(_dft1_kernel, so=so),
        grid=(nb, cols // tn),
        in_specs=[
            pl.BlockSpec(f1.shape, lambda b, j: (0, 0)),
            pl.BlockSpec((so, tn), lambda b, j: (blk_off + b, j)),
        ],
        out_specs=[pl.BlockSpec((so, tn), lambda b, j: (b, j))] * 2,
        out_shape=[jax.ShapeDtypeStruct((nb * so, cols), BF)] * 2,
        compiler_params=_params("arbitrary", "arbitrary"),
        name="dft1",
    )(f1, zc.reshape(n // si, cols))
    ka = min(so, max(1, 1024 // si))
    full = lambda a: pl.BlockSpec(a.shape, lambda t, b: (0,) * a.ndim)
    kt = so // ka
    out = pl.pallas_call(
        functools.partial(_dft2_kernel, ka=ka, si=si, w=w),
        grid=(kt, nb),
        in_specs=[
            pl.BlockSpec((ka, 2 * si, 2 * si), lambda t, b: (t, 0, 0)),
            pl.BlockSpec((ka * si, w), lambda t, b: (b * kt + t, 0)),
            pl.BlockSpec((ka * si, w), lambda t, b: (b * kt + t, 0)),
            full(cbd), full(sbd), full(fw),
        ],
        out_specs=pl.BlockSpec((si, ka * w), lambda t, b: (b, t)),
        out_shape=jax.ShapeDtypeStruct((nb * si, so * w), BF),
        scratch_shapes=[pltpu.VMEM((ka * si, w), BF), pltpu.VMEM((ka * si, w), BF)],
        compiler_params=_params("arbitrary", "arbitrary"),
        name="dft2",
    )(g, ar.reshape(nb * s, w), ai.reshape(nb * s, w), cbd, sbd, fw)
    return out.reshape(nb * s, w)


def _outproj_kernel(*refs, geo, pw, cw, split):
    x_refs, refs = (refs[:2], refs[2:]) if split else (refs[:1], refs[1:])
    (ma_ref, mb_ref, mc0_ref, mc1_ref, wo_ref, g1_ref, ng_ref, sc_ref, sh_ref,
     wr_ref, rb_ref, xo_ref, h_ref, lg_ref) = refs
    for rs in _row_chunks(geo.tm):
        x = _pick_group(geo, x_refs[0], x_refs[1], rs) if split else x_refs[0][rs, :]
        mc = _pick_group(geo, mc0_ref, mc1_ref, rs)
        y = _dot(ma_ref[rs, :], wo_ref[0:pw, :])
        y = y + _dot(mb_ref[rs, :], wo_ref[pw:pw + cw, :])
        y = y + _dot(mc, wo_ref[pw + cw:, :])
        x = x + g1_ref[...] * y
        xo_ref[rs, :] = x
        h = _rmsnorm(x, ng_ref[...]) * (1.0 + sc_ref[...]) + sh_ref[...]
        _pack_rows(h, h_ref, rs.start)
        lg_ref[rs, :] = _dot(h.astype(BF), wr_ref[...]) + rb_ref[...]


def _outproj(xs, ma, mb, mc0, mc1, w_out, mod, ng, wr, rb, geo):
    d = w_out.shape[1]
    tm = geo.tm
    pw, cw = ma.shape[1], mb.shape[1]
    fw = mc0.shape[1]
    split = len(xs) == 2
    x_specs = _two_group_specs(geo, d) if split else [pl.BlockSpec((tm, d), lambda i: (i, 0))]
    full = lambda a: pl.BlockSpec(a.shape, lambda i: (0,) * a.ndim)
    return pl.pallas_call(
        functools.partial(_outproj_kernel, geo=geo, pw=pw, cw=cw, split=split),
        grid=(geo.nt,),
        in_specs=x_specs + [
            pl.BlockSpec((tm, pw), lambda i: (i, 0)),
            pl.BlockSpec((tm, cw), lambda i: (i, 0)),
        ] + _two_group_specs(geo, fw) + [
            pl.BlockSpec(w_out.shape, lambda i: (0, 0), pipeline_mode=pl.Buffered(1)),
            _mod_spec(geo, d, 2),
            full(ng),
            _mod_spec(geo, d, 4),
            _mod_spec(geo, d, 3),
            full(wr), full(rb),
        ],
        out_specs=[
            pl.BlockSpec((tm, d), lambda i: (i, 0)),
            pl.BlockSpec((tm * _slab_rows(d), LANES), lambda i: (i, 0)),
            pl.BlockSpec((tm, LANES), lambda i: (i, 0)),
        ],
        out_shape=[
            jax.ShapeDtypeStruct((geo.n, d), F32),
            jax.ShapeDtypeStruct((geo.n * _slab_rows(d), LANES), jnp.uint32),
            jax.ShapeDtypeStruct((geo.n, LANES), F32),
        ],
        compiler_params=_params("arbitrary"),
        name="outproj",
    )(*xs, ma, mb, mc0, mc1, w_out, mod, ng, mod, mod, wr, rb)


def _route_kernel(lg_ref, info_ref, dest_ref, cnt_ref, carry_s, lower_s, *, n_exp, n_grp, blk):
    p = pl.program_id(0)
    i = pl.program_id(1)
    tm = lg_ref.shape[0]
    epg = n_exp // n_grp

    @pl.when((p == 0) & (i == 0))
    def _():
        carry_s[...] = jnp.zeros_like(carry_s)
        rr = lax.broadcasted_iota(jnp.int32, (tm, tm), 0)
        cc = lax.broadcasted_iota(jnp.int32, (tm, tm), 1)
        lower_s[...] = jnp.where(cc < rr, 1.0, 0.0).astype(BF)

    @pl.when((p == 1) & (i == 0))
    def _():
        cnt = carry_s[...]
        cnt_ref[...] = cnt
        nblk = jnp.floor((cnt + float(blk - 1)) * (1.0 / blk))
        hi = jnp.floor(nblk * (1.0 / 256.0))
        lo = nblk - 256.0 * hi
        r = lax.broadcasted_iota(jnp.int32, (LANES, LANES), 0)
        c = lax.broadcasted_iota(jnp.int32, (LANES, LANES), 1)
        upper = jnp.where(r < c, 1.0, 0.0).astype(BF)
        start_blk = 256.0 * _dot(hi.astype(BF), upper) + _dot(lo.astype(BF), upper)
        carry_s[...] = start_blk * float(blk)

    lg = lg_ref[...]
    lane = lax.broadcasted_iota(jnp.int32, (tm, LANES), 1).astype(F32)
    big = float(1 << 20)
    neg = -jnp.inf
    is_c = (lane >= n_exp) & (lane < n_exp + n_grp)
    lc = jnp.where(is_c, lg, neg)
    m = jnp.max(lc, axis=1, keepdims=True)
    gidx = jnp.min(jnp.where(lc == m, lane - n_exp, big), axis=1, keepdims=True)
    se = jnp.sum(jnp.where(is_c, jnp.exp(lc - m), 0.0), axis=1, keepdims=True)
    p_grp = 1.0 / se
    in_g = (lane >= gidx * epg) & (lane < (gidx + 1.0) * epg)
    lf = jnp.where(in_g, lg, neg)
    v1 = jnp.max(lf, axis=1, keepdims=True)
    i1 = jnp.min(jnp.where((lf == v1) & in_g, lane, big), axis=1, keepdims=True)
    in_g2 = in_g & (lane != i1)
    lf2 = jnp.where(in_g2, lg, neg)
    v2 = jnp.max(lf2, axis=1, keepdims=True)
    i2 = jnp.min(jnp.where((lf2 == v2) & in_g2, lane, big), axis=1, keepdims=True)
    e2 = jnp.exp(v2 - v1)
    den = 1.0 + e2
    w1 = p_grp * (1.0 / den)
    w2 = p_grp * (e2 / den)

    sel1 = lane == i1
    sel2 = lane == i2
    oh = jnp.where(sel1 | sel2, 1.0, 0.0)
    slot = carry_s[0:1, :] + _dot(lower_s[...], oh.astype(BF))
    d1 = jnp.sum(jnp.where(sel1, slot, 0.0), axis=1, keepdims=True)
    d2 = jnp.sum(jnp.where(sel2, slot, 0.0), axis=1, keepdims=True)
    carry_s[...] = carry_s[...] + jnp.sum(oh, axis=0, keepdims=True)

    @pl.when(p == 1)
    def _():
        info_ref[...] = jnp.where(lane == 0, w1, jnp.where(lane == 1, w2, 0.0))
        dd = jnp.where(lane == 0, d1, jnp.where(lane == 1, d2, 0.0))
        dt = jnp.transpose(dd)[0:SUBLANES, :].astype(jnp.int32)
        tt = dest_ref.shape[2]
        for j in range(dest_ref.shape[0]):
            dest_ref[j] = dt[:, j * tt:(j + 1) * tt]


def _route(lg, n_exp, n_grp, blk, tm):
    n = lg.shape[0]
    nt = n // tm
    per = math.gcd(nt, ROUTE_TILES)
    tr = per * tm
    assert blk & (blk - 1) == 0
    return pl.pallas_call(
        functools.partial(_route_kernel, n_exp=n_exp, n_grp=n_grp, blk=blk),
        grid=(2, n // tr),
        in_specs=[pl.BlockSpec((tr, LANES), lambda p, i: (i, 0))],
        out_specs=[
            pl.BlockSpec((tr, LANES), lambda p, i: (i * p, 0)),
            pl.BlockSpec((per, SUBLANES, tm), lambda p, i: (i * p, 0, 0)),
            pl.BlockSpec((SUBLANES, LANES), lambda p, i: (0, 0)),
        ],
        out_shape=[
            jax.ShapeDtypeStruct((n, LANES), F32),
            jax.ShapeDtypeStruct((nt, SUBLANES, tm), jnp.int32),
            jax.ShapeDtypeStruct((SUBLANES, LANES), F32),
        ],
        scratch_shapes=[pltpu.VMEM((SUBLANES, LANES), F32), pltpu.VMEM((tr, tr), BF)],
        compiler_params=_params("arbitrary", "arbitrary"),
        name="route",
    )(lg)


def _dispatch_kernel(cnt_ref, pst_ref, nu_ref, dest_hbm, h_ref, xs_hbm, dsm, isem, sem, zblk, *, blk, n_exp, ns):
    i = pl.program_id(0)
    per, _, tm = dsm.shape
    cp = pltpu.make_async_copy(dest_hbm.at[pl.ds(i * per, per)], dsm, isem)
    cp.start()
    cp.wait()

    def row_copy(j, t, k):
        return pltpu.make_async_copy(_slab_at(h_ref, j * tm + t, ns), _slab_at(xs_hbm, dsm[j, k, t], ns), sem)

    for j in range(per):
        def issue(tb, c):
            for u in range(DMA_UNROLL):
                for k in range(TOP_K):
                    row_copy(j, tb * DMA_UNROLL + u, k).start(priority=k % 2)
            return c

        lax.fori_loop(0, tm // DMA_UNROLL, issue, 0)

    for j in range(per):
        def drain(tb, c):
            for u in range(DMA_UNROLL):
                for k in range(TOP_K):
                    row_copy(j, tb * DMA_UNROLL + u, k).wait()
            return c

        lax.fori_loop(0, tm // DMA_UNROLL, drain, 0)

    @pl.when(i == pl.num_programs(0) - 1)
    def _():
        zblk[...] = jnp.zeros_like(zblk)

        def per_expert(e, c):
            cnt = cnt_ref[e]
            first = pst_ref[e] + cnt
            n_pad = (blk - cnt % blk) % blk

            def pad_copy(r):
                return pltpu.make_async_copy(_slab_at(zblk, 0, ns), _slab_at(xs_hbm, first + r, ns), sem)

            def pad_start(r, c2):
                pad_copy(r).start()
                return c2

            def pad_wait(r, c2):
                pad_copy(r).wait()
                return c2

            lax.fori_loop(0, n_pad, pad_start, 0)
            lax.fori_loop(0, n_pad, pad_wait, 0)
            return c

        lax.fori_loop(0, n_exp, per_expert, 0)

        def unused_block(b, c):
            z = pltpu.make_async_copy(zblk, _slab_at(xs_hbm, b, blk * ns), sem)
            z.start()
            z.wait()
            return c

        lax.fori_loop(nu_ref[0], xs_hbm.shape[0] // (blk * ns), unused_block, 0)


def _dispatch(cnt, pstart, n_used, dest, h, n_rows, ns, blk, tm):
    n = h.shape[0] // ns
    n_exp = cnt.shape[0]
    per = math.gcd(n // tm, ROUTE_TILES)
    return pl.pallas_call(
        functools.partial(_dispatch_kernel, blk=blk, n_exp=n_exp, ns=ns),
        grid_spec=pltpu.PrefetchScalarGridSpec(
            num_scalar_prefetch=3,
            grid=(n // (per * tm),),
            in_specs=[
                pl.BlockSpec(memory_space=pl.ANY),
                pl.BlockSpec((per * tm * ns, LANES), lambda i, c, p, u: (i, 0)),
            ],
            out_specs=pl.BlockSpec(memory_space=pl.ANY),
            scratch_shapes=[
                pltpu.SMEM((per, SUBLANES, tm), jnp.int32),
                pltpu.SemaphoreType.DMA,
                pltpu.SemaphoreType.DMA,
                pltpu.VMEM((blk * ns, LANES), h.dtype),
            ],
        ),
        out_shape=jax.ShapeDtypeStruct((n_rows * ns, LANES), h.dtype),
        compiler_params=_params("arbitrary"),
        name="dispatch",
    )(cnt, pstart, n_used, dest, h)


def _expert_kernel(be_ref, bi_ref, nu_ref, xs_ref, w1_ref, w3_ref, w2_ref, ys_ref, w1_s, w3_s, w2_s, *, blk):
    b = pl.program_id(0)
    used = b < nu_ref[0]

    @pl.when(jnp.logical_not(used))
    def _():
        ys_ref[...] = jnp.zeros_like(ys_ref)

    @pl.when(used & ((b == 0) | (be_ref[b] != be_ref[jnp.maximum(b - 1, 0)])))
    def _():
        w1_s[...] = w1_ref[...].astype(BF)
        w3_s[...] = w3_ref[...].astype(BF)
        w2_s[...] = w2_ref[...].astype(BF)

    @pl.when(used)
    def _():
        x = _unpack_rows(xs_ref, blk, BF)
        de = w1_s.shape[1]
        hms = []
        for cs in (slice(0, de // 2), slice(de // 2, de)):
            a = _dot(x, w1_s[:, cs])
            c = _dot(x, w3_s[:, cs])
            hms.append((a * jax.nn.sigmoid(a) * c).astype(BF))
        hm = jnp.concatenate(hms, axis=1)
        d = w2_s.shape[1]
        gw = 2 * LANES
        for g in range(d // 2 // gw):
            lo = _dot(hm, w2_s[:, g * gw:(g + 1) * gw])
            hi = _dot(hm, w2_s[:, d // 2 + g * gw:d // 2 + (g + 1) * gw])
            _pack_cols(lo, hi, ys_ref, g * (gw // LANES))


def _experts(blk_e, blk_i, n_used, xs, w1, w3, w2, layer, blk):
    d, de = w1.shape[2:]
    ns = _slab_rows(d)
    w_idx = lambda b, be, bi, nu: (layer, be[b], 0, 0)
    return pl.pallas_call(
        functools.partial(_expert_kernel, blk=blk),
        grid_spec=pltpu.PrefetchScalarGridSpec(
            num_scalar_prefetch=3,
            grid=(xs.shape[0] // (blk * ns),),
            in_specs=[
                pl.BlockSpec((blk * ns, LANES), lambda b, be, bi, nu: (bi[b], 0)),
                pl.BlockSpec((None, None, d, de), w_idx),
                pl.BlockSpec((None, None, d, de), w_idx),
                pl.BlockSpec((None, None, de, d), w_idx),
            ],
            out_specs=pl.BlockSpec((blk * ns, LANES), lambda b, be, bi, nu: (b, 0)),
            scratch_shapes=[pltpu.VMEM((d, de), BF), pltpu.VMEM((d, de), BF), pltpu.VMEM((de, d), BF)],
        ),
        out_shape=jax.ShapeDtypeStruct(xs.shape, xs.dtype),
        compiler_params=_params("arbitrary"),
        name="experts",
    )(blk_e, blk_i, n_used, xs, w1, w3, w2)


def _combine_kernel(*refs, geo, final):
    if final:
        dest_hbm, ys_hbm, x_ref, info_ref, g2_ref, fg_ref, o0_ref, o1_ref = refs[:8]
    else:
        dest_hbm, ys_hbm, x_ref, info_ref, g2_ref, o_ref = refs[:6]
    dsm, isem, sems, b00, b01, b10, b11 = refs[-7:]
    bufs = ((b00, b01), (b10, b11))
    i = pl.program_id(0)
    nt = pl.num_programs(0)
    tm, d = x_ref.shape
    ns = _slab_rows(d)

    def row_copy(slot, t, k):
        return pltpu.make_async_copy(_slab_at(ys_hbm, dsm[slot, k, t], ns), _slab_at(bufs[slot][k], t, ns),
                                     sems.at[slot])

    def start_gather(tile, slot):
        cp = pltpu.make_async_copy(dest_hbm.at[tile], dsm.at[slot], isem)
        cp.start()
        cp.wait()

        def issue(tb, c):
            for u in range(DMA_UNROLL):
                for k in range(TOP_K):
                    row_copy(slot, tb * DMA_UNROLL + u, k).start(priority=k % 2)
            return c

        lax.fori_loop(0, tm // DMA_UNROLL, issue, 0)

    def finish_gather(slot):
        def drain(tb, c):
            for u in range(DMA_UNROLL):
                for k in range(TOP_K):
                    row_copy(slot, tb * DMA_UNROLL + u, k).wait()
            return c

        lax.fori_loop(0, tm // DMA_UNROLL, drain, 0)

    @pl.when(i == 0)
    def _():
        start_gather(0, 0)

    def step(slot):
        @pl.when(i + 1 < nt)
        def _():
            start_gather(i + 1, 1 - slot)

        finish_gather(slot)
        y = (_unpack_rows(bufs[slot][0], tm, F32) * info_ref[:, 0:1]
             + _unpack_rows(bufs[slot][1], tm, F32) * info_ref[:, 1:2])
        x = x_ref[...] + g2_ref[...] * y
        if final:
            x = _rmsnorm(x, fg_ref[...])

            @pl.when(i < geo.n0t)
            def _():
                o0_ref[...] = x

            @pl.when(i >= geo.n0t)
            def _():
                o1_ref[...] = x
        else:
            o_ref[...] = x

    for slot in range(2):
        pl.when(i % 2 == slot)(functools.partial(step, slot))


def _combine(dest, ys, x, info, mod, geo, final_g=None):
    n, d = x.shape
    tm = geo.tm
    final = final_g is not None
    in_specs = [
        pl.BlockSpec(memory_space=pl.ANY),
        pl.BlockSpec(memory_space=pl.ANY),
        pl.BlockSpec((tm, d), lambda i: (i, 0)),
        pl.BlockSpec((tm, LANES), lambda i: (i, 0)),
        _mod_spec(geo, d, 5),
    ]
    args = [dest, ys, x, info, mod]
    if final:
        in_specs.append(pl.BlockSpec((1, d), lambda i: (0, 0)))
        args.append(final_g)
        out_specs = _two_group_specs(geo, d)
        out_shape = [jax.ShapeDtypeStruct((geo.n0, d), F32), jax.ShapeDtypeStruct((geo.n1, d), F32)]
    else:
        out_specs = pl.BlockSpec((tm, d), lambda i: (i, 0))
        out_shape = jax.ShapeDtypeStruct((n, d), F32)
    return pl.pallas_call(
        functools.partial(_combine_kernel, geo=geo, final=final),
        grid=(geo.nt,),
        in_specs=in_specs,
        out_specs=out_specs,
        out_shape=out_shape,
        scratch_shapes=[
            pltpu.SMEM((2, SUBLANES, tm), jnp.int32),
            pltpu.SemaphoreType.DMA,
            pltpu.SemaphoreType.DMA((2,)),
        ] + [pltpu.VMEM((tm * _slab_rows(d), LANES), ys.dtype)] * (2 * TOP_K),
        compiler_params=_params("arbitrary"),
        name="combine",
    )(*args)


def _block_tables(cnt_f, n_exp, blk, n_blocks):
    cnt = cnt_f[0, :n_exp].astype(jnp.int32)
    nblk = (cnt + (blk - 1)) // blk
    end_blk = jnp.cumsum(nblk)
    n_used = end_blk[-1]
    pstart = (end_blk - nblk) * blk
    b = jnp.minimum(jnp.arange(n_blocks, dtype=jnp.int32), n_used - 1)
    blk_e = jnp.minimum(jnp.sum(end_blk[None, :] <= b[:, None], axis=1), n_exp - 1).astype(jnp.int32)
    return cnt, pstart.astype(jnp.int32), blk_e, b, n_used.reshape(1).astype(jnp.int32)


def kernel(x_prompt, x_sample, c_prompt, c_sample, w_ada, b_ada, norm1_g, w_in, pool_w, pool_scale, conv_dw_w, conv_dw_b, conv_ln_g, conv_ln_b, conv_pw_w, conv_pw_b, fourier_w, w_out, norm2_g, router_coarse_w, router_coarse_b, router_fine_w, router_fine_b, expert_w1, expert_w3, expert_w2, final_g):
    b0, s0, d = x_prompt.shape
    b1, s1, _ = x_sample.shape
    depth = w_ada.shape[0]
    pw = pool_w.shape[1] * pool_w.shape[2]
    cw = conv_pw_w.shape[1]
    fw = fourier_w.shape[1]
    n_grp = router_coarse_w.shape[2]
    epg = router_fine_w.shape[3]
    n_exp = n_grp * epg
    assert n_exp + n_grp <= LANES and pool_w.shape[1] == len(POOL_WINDOWS) and d % (2 * LANES) == 0
    tm = math.gcd(math.gcd(s0, s1), 256)
    geo = _Geom(b0, s0, b1, s1, tm)
    geo_mm = _Geom(b0, s0, b1, s1, math.gcd(math.gcd(s0, s1), MM_ROWS))
    n = geo.n
    n_seq = b0 + b1
    blk = ROUTE_ROWS
    n_blocks = -(-(n * TOP_K + n_exp * (blk - 1)) // blk)

    c = jnp.concatenate([c_prompt, c_sample], axis=0)
    mod = _ada(c, w_ada, b_ada).reshape(depth, n_seq, 6, 1, d)

    xs = [x_prompt.reshape(geo.n0, d), x_sample.reshape(geo.n1, d)]
    for l in range(depth):
        ml = mod[l]
        za, v, zc = _inproj(xs, norm1_g[l].reshape(1, d), ml, w_in[l].astype(BF), geo_mm, pw, cw, fw)
        ma, mb = _mix(za, v, pool_w[l].astype(BF), pool_scale[l].reshape(1, pw), conv_dw_w[l],
                      conv_dw_b[l].reshape(1, cw), conv_ln_g[l].reshape(1, cw), conv_ln_b[l].reshape(1, cw),
                      conv_pw_w[l].astype(BF), conv_pw_b[l].reshape(1, cw), geo)
        fwl = fourier_w[l].astype(BF)
        mc0 = _fourier(zc, fwl, b0, s0, 0)
        mc1 = _fourier(zc, fwl, b1, s1, geo.n0)

        wr = jnp.concatenate([
            jnp.transpose(router_fine_w[l], (1, 0, 2)).reshape(d, n_exp),
            router_coarse_w[l],
            jnp.zeros((d, LANES - n_exp - n_grp), F32)], axis=1)
        rb = jnp.concatenate([router_fine_b[l].reshape(n_exp), router_coarse_b[l],
                              jnp.zeros((LANES - n_exp - n_grp,), F32)]).reshape(1, LANES)
        x, h, lg = _outproj(xs, ma, mb, mc0, mc1, w_out[l].astype(BF), ml, norm2_g[l].reshape(1, d),
                            wr.astype(BF), rb, geo_mm)

        info, dest, cnt_f = _route(lg, n_exp, n_grp, blk, tm)
        cnt, pstart, blk_e, blk_i, n_used = _block_tables(cnt_f, n_exp, blk, n_blocks)
        xsort = _dispatch(cnt, pstart, n_used, dest, h, n_blocks * blk, _slab_rows(d), blk, tm)
        ys = _experts(blk_e, blk_i, n_used, xsort, expert_w1, expert_w3, expert_w2, l, blk)
        if l + 1 < depth:
            xs = [_combine(dest, ys, x, info, ml, geo)]
        else:
            y0, y1 = _combine(dest, ys, x, info, ml, geo, final_g.reshape(1, d))
    return (y0.reshape(b0, s0, d), y1.reshape(b1, s1, d))
```

```python
import functools
import math

import numpy as np
import jax
import jax.numpy as jnp
from jax import lax
from jax.experimental import pallas as pl
from jax.experimental.pallas import tpu as pltpu

EPS = 1e-6
POOL_WINDOWS = (2, 4, 8, 16)
N_FOURIER = 4
TOP_K = 2
BF = jnp.bfloat16
F32 = jnp.float32

LANES = 128
SUBLANES = 8
BF16_ROWS = 16
VMEM_LIMIT = 56 * 1024 * 1024
ROUTE_ROWS = 512
CONV_ROWS = 64
DMA_UNROLL = 8
MM_ROWS = 512
MM_CHUNK = 256
ROUTE_TILES = 4


def _dot(a, b):
    return jnp.dot(a, b, preferred_element_type=F32)


def _params(*sem):
    return pltpu.CompilerParams(dimension_semantics=sem, vmem_limit_bytes=VMEM_LIMIT)


class _Geom:
    def __init__(self, b0, s0, b1, s1, tm):
        assert s0 % tm == 0 and s1 % tm == 0
        self.b0, self.s0, self.b1, self.s1, self.tm = b0, s0, b1, s1, tm
        self.n0, self.n1 = b0 * s0, b1 * s1
        self.n = self.n0 + self.n1
        self.n0t, self.nt = self.n0 // tm, self.n // tm
        self.tps0, self.tps1 = s0 // tm, s1 // tm

    def seq_of_tile(self, i):
        return jnp.where(i < self.n0t, i // self.tps0, self.b0 + (i - self.n0t) // self.tps1)

    def tile_in_seq(self, i):
        tin = jnp.where(i < self.n0t, i % self.tps0, (i - self.n0t) % self.tps1)
        tps = jnp.where(i < self.n0t, self.tps0, self.tps1)
        return tin, tps


def _rmsnorm(x, g):
    return x * lax.rsqrt(jnp.mean(x * x, axis=-1, keepdims=True) + EPS) * g


HI_MASK = 0xFFFF0000


def _slab_rows(d):
    return d // (2 * LANES)


def _pack_rows(x, slab_ref, row0=0):
    rows, d = x.shape
    half, ns = d // 2, _slab_rows(d)
    bits = pltpu.bitcast(x.astype(BF).astype(F32), jnp.uint32)
    words = (bits[:, :half] >> 16) | (bits[:, half:] & jnp.uint32(HI_MASK))
    for s in range(ns):
        slab_ref[pl.ds(row0 * ns + s, rows, stride=ns), :] = words[:, s * LANES:(s + 1) * LANES]


def _pack_cols(lo, hi, slab_ref, s0):
    rows = lo.shape[0]
    ns = slab_ref.shape[0] // rows
    lo_bits = pltpu.bitcast(lo.astype(BF).astype(F32), jnp.uint32)
    hi_bits = pltpu.bitcast(hi.astype(BF).astype(F32), jnp.uint32)
    words = (lo_bits >> 16) | (hi_bits & jnp.uint32(HI_MASK))
    for j in range(lo.shape[1] // LANES):
        slab_ref[pl.ds(s0 + j, rows, stride=ns), :] = words[:, j * LANES:(j + 1) * LANES]


def _unpack_rows(slab_ref, rows, dtype):
    ns = slab_ref.shape[0] // rows
    lo, hi = [], []
    for s in range(ns):
        words = slab_ref[pl.ds(s, rows, stride=ns), :]
        lo.append(pltpu.bitcast(words << 16, F32).astype(dtype))
        hi.append(pltpu.bitcast(words & jnp.uint32(HI_MASK), F32).astype(dtype))
    return jnp.concatenate(lo + hi, axis=1)


def _slab_at(ref, row, ns):
    return ref.at[pl.ds(pl.multiple_of(row * ns, ns), ns), :]


def _ada_kernel(c_ref, w_ref, b_ref, o_ref):
    c = c_ref[...]
    s = (c * jax.nn.sigmoid(c)).astype(BF)
    o_ref[...] = _dot(s, w_ref[...].astype(BF)) + b_ref[...]


def _ada(c, w_ada, b_ada):
    depth, d, d6 = w_ada.shape
    nb = c.shape[0]
    tn = min(1024, d6)
    return pl.pallas_call(
        _ada_kernel,
        grid=(depth, d6 // tn),
        in_specs=[
            pl.BlockSpec((nb, d), lambda l, j: (0, 0)),
            pl.BlockSpec((None, d, tn), lambda l, j: (l, 0, j)),
            pl.BlockSpec((None, 1, tn), lambda l, j: (l, 0, j)),
        ],
        out_specs=pl.BlockSpec((None, nb, tn), lambda l, j: (l, 0, j)),
        out_shape=jax.ShapeDtypeStruct((depth, nb, d6), F32),
        compiler_params=_params("arbitrary", "arbitrary"),
        name="ada",
    )(c, w_ada, b_ada.reshape(depth, 1, d6))


def _mod_spec(geo, d, j):
    return pl.BlockSpec((None, None, 1, d), lambda i: (geo.seq_of_tile(i), j, 0, 0))


def _two_group_specs(geo, width):
    tm = geo.tm
    return [
        pl.BlockSpec((tm, width), lambda i: (jnp.minimum(i, geo.n0t - 1), 0)),
        pl.BlockSpec((tm, width), lambda i: (jnp.maximum(i - geo.n0t, 0), 0)),
    ]


def _pick_group(geo, ref0, ref1, rows=slice(None)):
    return jnp.where(pl.program_id(0) < geo.n0t, ref0[rows, :], ref1[rows, :])


def _row_chunks(tm):
    cr = min(tm, MM_CHUNK)
    return [slice(r, r + cr) for r in range(0, tm, cr)]


def _inproj_kernel(*refs, geo, pw, cw, split):
    if split:
        x0_ref, x1_ref, g_ref, sc_ref, sh_ref, w_ref, za_ref, v_ref, zc_ref = refs
    else:
        x_ref, g_ref, sc_ref, sh_ref, w_ref, za_ref, v_ref, zc_ref = refs
    for rs in _row_chunks(geo.tm):
        x = _pick_group(geo, x0_ref, x1_ref, rs) if split else x_ref[rs, :]
        h = _rmsnorm(x, g_ref[...]) * (1.0 + sc_ref[...]) + sh_ref[...]
        hb = h.astype(BF)
        za_ref[rs, :] = _dot(hb, w_ref[:, :pw])
        a = _dot(hb, w_ref[:, pw:pw + cw])
        b = _dot(hb, w_ref[:, pw + cw:pw + 2 * cw])
        v_ref[rs, :] = (a * jax.nn.sigmoid(b)).astype(BF)
        zc_ref[rs, :] = _dot(hb, w_ref[:, pw + 2 * cw:]).astype(BF)


def _inproj(xs, g, mod, w_in, geo, pw, cw, fw):
    d = w_in.shape[0]
    tm = geo.tm
    split = len(xs) == 2
    x_specs = _two_group_specs(geo, d) if split else [pl.BlockSpec((tm, d), lambda i: (i, 0))]
    return pl.pallas_call(
        functools.partial(_inproj_kernel, geo=geo, pw=pw, cw=cw, split=split),
        grid=(geo.nt,),
        in_specs=x_specs + [
            pl.BlockSpec((1, d), lambda i: (0, 0)),
            _mod_spec(geo, d, 1),
            _mod_spec(geo, d, 0),
            pl.BlockSpec(w_in.shape, lambda i: (0, 0), pipeline_mode=pl.Buffered(1)),
        ],
        out_specs=[
            pl.BlockSpec((tm, pw), lambda i: (i, 0)),
            pl.BlockSpec((tm, cw), lambda i: (i, 0)),
            pl.BlockSpec((tm, fw), lambda i: (i, 0)),
        ],
        out_shape=[
            jax.ShapeDtypeStruct((geo.n, pw), F32),
            jax.ShapeDtypeStruct((geo.n, cw), BF),
            jax.ShapeDtypeStruct((geo.n, fw), BF),
        ],
        compiler_params=_params("arbitrary"),
        name="inproj",
    )(*xs, g, mod, mod, w_in)


def _mix_kernel(za_ref, zap_ref, zan_ref, v_ref, vp_ref, vn_ref, pw_ref, ps_ref,
                dw_ref, dwb_ref, lng_ref, lnb_ref, pww_ref, pwb_ref,
                ma_ref, mb_ref, za_s, vs_s, cv_s, *, geo, n_taps):
    tm = geo.tm
    i = pl.program_id(0)
    tin, tps = geo.tile_in_seq(i)
    is_first = tin == 0
    is_last = tin == tps - 1
    pos0 = tin * tm
    seq_len = tps * tm
    ha = SUBLANES
    hv = BF16_ROWS
    half = n_taps // 2

    za = za_ref[...]
    za_s[0:ha, :] = jnp.where(is_first, 0.0, zap_ref[...])
    za_s[ha:ha + tm, :] = za
    za_s[ha + tm:, :] = jnp.where(is_last, 0.0, zan_ref[...])
    t = pos0 + lax.broadcasted_iota(jnp.int32, (tm, 1), 0)
    n_pool = pw_ref.shape[0]
    pc = pw_ref.shape[1]
    for gi in range(n_pool):
        w = POOL_WINDOWS[gi]
        lo, hi = -(w // 2), w - w // 2
        c0 = gi * pc
        acc = za_s[ha + lo:ha + lo + tm, c0:c0 + pc]
        for o in range(lo + 1, hi):
            acc = acc + za_s[ha + o:ha + o + tm, c0:c0 + pc]
        cnt = (jnp.minimum(t + hi, seq_len) - jnp.maximum(t + lo, 0)).astype(F32)
        dlt = acc / cnt - za[:, c0:c0 + pc]
        y = _dot(dlt.astype(BF), pw_ref[gi]) * ps_ref[:, c0:c0 + pc]
        ma_ref[:, c0:c0 + pc] = y.astype(BF)

    cw = v_ref.shape[1]
    rows = tm + 2 * hv
    vs_s[0, 0:hv, :] = jnp.where(is_first, 0.0, vp_ref[...].astype(F32))
    vs_s[0, hv:hv + tm, :] = v_ref[...].astype(F32)
    vs_s[0, hv + tm:rows, :] = jnp.where(is_last, 0.0, vn_ref[...].astype(F32))
    shifted = rows - SUBLANES
    for r in range(1, SUBLANES):
        vs_s[r, 0:shifted, :] = vs_s[0, r:r + shifted, :]

    base = hv - half

    def conv_rows(rb, carry):
        r0 = pl.multiple_of(rb * CONV_ROWS, CONV_ROWS)
        for cb in range(cw // LANES):
            c0 = cb * LANES
            acc = jnp.broadcast_to(dwb_ref[:, c0:c0 + LANES], (CONV_ROWS, LANES))
            for r in range(SUBLANES):
                qs = [q for q in range(-(-(n_taps + base) // SUBLANES)) if 0 <= q * SUBLANES + r - base < n_taps]
                if not qs:
                    continue
                span = CONV_ROWS + (qs[-1] - qs[0]) * SUBLANES
                rows = vs_s[r, pl.ds(r0 + qs[0] * SUBLANES, span), c0:c0 + LANES]
                for q in qs:
                    k = q * SUBLANES + r - base
                    off = (q - qs[0]) * SUBLANES
                    acc = acc + rows[off:off + CONV_ROWS, :] * dw_ref[k:k + 1, c0:c0 + LANES]
            cv_s[pl.ds(r0, CONV_ROWS), c0:c0 + LANES] = acc
        return carry

    lax.fori_loop(0, tm // CONV_ROWS, conv_rows, 0)

    u = cv_s[...]
    uc = u - jnp.mean(u, axis=-1, keepdims=True)
    var = jnp.mean(uc * uc, axis=-1, keepdims=True)
    y = uc * lax.rsqrt(var + EPS) * lng_ref[...] + lnb_ref[...]
    y = y * jax.nn.sigmoid(y)
    mb_ref[...] = (_dot(y.astype(BF), pww_ref[...]) + pwb_ref[...]).astype(BF)


def _mix(za, v, pool_w, pool_scale, dw_w, dw_b, ln_g, ln_b, pw_w, pw_b, geo):
    tm = geo.tm
    n, pw = za.shape
    cw = v.shape[1]
    n_taps = dw_w.shape[0]
    assert n_taps // 2 < BF16_ROWS and max(POOL_WINDOWS) // 2 <= SUBLANES
    assert tm % CONV_ROWS == 0 and tm % BF16_ROWS == 0
    ra, rv = tm // SUBLANES, tm // BF16_ROWS
    full = lambda a: pl.BlockSpec(a.shape, lambda i: (0,) * a.ndim)
    return pl.pallas_call(
        functools.partial(_mix_kernel, geo=geo, n_taps=n_taps),
        grid=(geo.nt,),
        in_specs=[
            pl.BlockSpec((tm, pw), lambda i: (i, 0)),
            pl.BlockSpec((SUBLANES, pw), lambda i: (jnp.maximum(i * ra - 1, 0), 0)),
            pl.BlockSpec((SUBLANES, pw), lambda i: (jnp.minimum((i + 1) * ra, n // SUBLANES - 1), 0)),
            pl.BlockSpec((tm, cw), lambda i: (i, 0)),
            pl.BlockSpec((BF16_ROWS, cw), lambda i: (jnp.maximum(i * rv - 1, 0), 0)),
            pl.BlockSpec((BF16_ROWS, cw), lambda i: (jnp.minimum((i + 1) * rv, n // BF16_ROWS - 1), 0)),
            full(pool_w), full(pool_scale), full(dw_w), full(dw_b), full(ln_g), full(ln_b),
            full(pw_w), full(pw_b),
        ],
        out_specs=[
            pl.BlockSpec((tm, pw), lambda i: (i, 0)),
            pl.BlockSpec((tm, cw), lambda i: (i, 0)),
        ],
        out_shape=[jax.ShapeDtypeStruct((n, pw), BF), jax.ShapeDtypeStruct((n, cw), BF)],
        scratch_shapes=[
            pltpu.VMEM((tm + 2 * SUBLANES, pw), F32),
            pltpu.VMEM((SUBLANES, tm + 2 * BF16_ROWS, cw), F32),
            pltpu.VMEM((tm, cw), F32),
        ],
        compiler_params=_params("arbitrary"),
        name="mix",
    )(za, za, za, v, v, v, pool_w, pool_scale, dw_w, dw_b, ln_g, ln_b, pw_w, pw_b)


def _dft_split(s):
    so = 1 << ((int(math.log2(s)) + 1) // 2)
    so = max(min(so, 128), 8)
    assert s % so == 0
    return so, s // so


def _dft_tables(s, so, si, ch):
    k = np.arange(so)
    ang1 = 2.0 * np.pi * ((k[:, None] * k[None, :]) % so) / so
    f1 = np.concatenate([np.cos(ang1), -np.sin(ang1)], axis=0)
    ka = np.arange(so)[:, None, None]
    kb = np.arange(si)[None, :, None]
    ni = np.arange(si)[None, None, :]
    num = (ni * kb * so + ni * ka) % s
    ang2 = 2.0 * np.pi * num / s
    gr = np.cos(ang2) / np.sqrt(s)
    gi = -np.sin(ang2) / np.sqrt(s)
    g = np.concatenate([np.concatenate([gr, -gi], axis=2),
                        np.concatenate([gi, gr], axis=2)], axis=1)
    c = np.arange(ch)
    angc = 2.0 * np.pi * ((c[:, None] * c[None, :]) % ch) / ch
    eye = np.eye(N_FOURIER)
    cbd = np.kron(eye, np.cos(angc) / np.sqrt(ch))
    sbd = np.kron(eye, np.sin(angc) / np.sqrt(ch))
    as_bf = lambda a: jnp.asarray(a.astype(np.float32)).astype(BF)
    return as_bf(f1), as_bf(g), as_bf(cbd), as_bf(sbd)


def _dft1_kernel(f_ref, z_ref, ar_ref, ai_ref, *, so):
    r = _dot(f_ref[...], z_ref[...])
    ar_ref[...] = r[:so].astype(BF)
    ai_ref[...] = r[so:].astype(BF)


def _dft2_kernel(g_ref, ar_ref, ai_ref, cbd_ref, sbd_ref, fw_ref, o_ref, zr_s, zi_s, *, ka, si, w):
    for j in range(ka):
        a = jnp.concatenate([ar_ref[j * si:(j + 1) * si, :], ai_ref[j * si:(j + 1) * si, :]], axis=0)
        z = _dot(g_ref[j], a)
        zr_s[j * si:(j + 1) * si, :] = z[:si].astype(BF)
        zi_s[j * si:(j + 1) * si, :] = z[si:].astype(BF)
    rz = _dot(zr_s[...], cbd_ref[...]) + _dot(zi_s[...], sbd_ref[...])
    out = _dot(rz.astype(BF), fw_ref[...]).astype(BF)
    for j in range(ka):
        o_ref[:, j * w:(j + 1) * w] = out[j * si:(j + 1) * si, :]


def _fourier(zc, fw, nb, s, row_off):
    n, w = zc.shape
    so, si = _dft_split(s)
    assert row_off % s == 0 and si % BF16_ROWS == 0
    f1, g, cbd, sbd = _dft_tables(s, so, si, w // N_FOURIER)
    blk_off = row_off // s
    cols = si * w
    tn = min(cols, 4096)
    ar, ai = pl.pallas_call(
        functools.partial(_dft1_kernel, so=so),
        grid=(nb, cols // tn),
        in_specs=[
            pl.BlockSpec(f1.shape, lambda b, j: (0, 0)),
            pl.BlockSpec((so, tn), lambda b, j: (blk_off + b, j)),
        ],
        out_specs=[pl.BlockSpec((so, tn), lambda b, j: (b, j))] * 2,
        out_shape=[jax.ShapeDtypeStruct((nb * so, cols), BF)] * 2,
        compiler_params=_params("arbitrary", "arbitrary"),
        name="dft1",
    )(f1, zc.reshape(n // si, cols))
    ka = min(so, max(1, 1024 // si))
    full = lambda a: pl.BlockSpec(a.shape, lambda t, b: (0,) * a.ndim)
    kt = so // ka
    out = pl.pallas_call(
        functools.partial(_dft2_kernel, ka=ka, si=si, w=w),
        grid=(kt, nb),
        in_specs=[
            pl.BlockSpec((ka, 2 * si, 2 * si), lambda t, b: (t, 0, 0)),
            pl.BlockSpec((ka * si, w), lambda t, b: (b * kt + t, 0)),
            pl.BlockSpec((ka * si, w), lambda t, b: (b * kt + t, 0)),
            full(cbd), full(sbd), full(fw),
        ],
        out_specs=pl.BlockSpec((si, ka * w), lambda t, b: (b, t)),
        out_shape=jax.ShapeDtypeStruct((nb * si, so * w), BF),
        scratch_shapes=[pltpu.VMEM((ka * si, w), BF), pltpu.VMEM((ka * si, w), BF)],
        compiler_params=_params("arbitrary", "arbitrary"),
        name="dft2",
    )(g, ar.reshape(nb * s, w), ai.reshape(nb * s, w), cbd, sbd, fw)
    return out.reshape(nb * s, w)


def _outproj_kernel(*refs, geo, pw, cw, split):
    x_refs, refs = (refs[:2], refs[2:]) if split else (refs[:1], refs[1:])
    (ma_ref, mb_ref, mc0_ref, mc1_ref, wo_ref, g1_ref, ng_ref, sc_ref, sh_ref,
     wr_ref, rb_ref, xo_ref, h_ref, lg_ref) = refs
    for rs in _row_chunks(geo.tm):
        x = _pick_group(geo, x_refs[0], x_refs[1], rs) if split else x_refs[0][rs, :]
        mc = _pick_group(geo, mc0_ref, mc1_ref, rs)
        y = _dot(ma_ref[rs, :], wo_ref[0:pw, :])
        y = y + _dot(mb_ref[rs, :], wo_ref[pw:pw + cw, :])
        y = y + _dot(mc, wo_ref[pw + cw:, :])
        x = x + g1_ref[...] * y
        xo_ref[rs, :] = x
        h = _rmsnorm(x, ng_ref[...]) * (1.0 + sc_ref[...]) + sh_ref[...]
        _pack_rows(h, h_ref, rs.start)
        lg_ref[rs, :] = _dot(h.astype(BF), wr_ref[...]) + rb_ref[...]


def _outproj(xs, ma, mb, mc0, mc1, w_out, mod, ng, wr, rb, geo):
    d = w_out.shape[1]
    tm = geo.tm
    pw, cw = ma.shape[1], mb.shape[1]
    fw = mc0.shape[1]
    split = len(xs) == 2
    x_specs = _two_group_specs(geo, d) if split else [pl.BlockSpec((tm, d), lambda i: (i, 0))]
    full = lambda a: pl.BlockSpec(a.shape, lambda i: (0,) * a.ndim)
    return pl.pallas_call(
        functools.partial(_outproj_kernel, geo=geo, pw=pw, cw=cw, split=split),
        grid=(geo.nt,),
        in_specs=x_specs + [
            pl.BlockSpec((tm, pw), lambda i: (i, 0)),
            pl.BlockSpec((tm, cw), lambda i: (i, 0)),
        ] + _two_group_specs(geo, fw) + [
            pl.BlockSpec(w_out.shape, lambda i: (0, 0), pipeline_mode=pl.Buffered(1)),
            _mod_spec(geo, d, 2),
            full(ng),
            _mod_spec(geo, d, 4),
            _mod_spec(geo, d, 3),
            full(wr), full(rb),
        ],
        out_specs=[
            pl.BlockSpec((tm, d), lambda i: (i, 0)),
            pl.BlockSpec((tm * _slab_rows(d), LANES), lambda i: (i, 0)),
            pl.BlockSpec((tm, LANES), lambda i: (i, 0)),
        ],
        out_shape=[
            jax.ShapeDtypeStruct((geo.n, d), F32),
            jax.ShapeDtypeStruct((geo.n * _slab_rows(d), LANES), jnp.uint32),
            jax.ShapeDtypeStruct((geo.n, LANES), F32),
        ],
        compiler_params=_params("arbitrary"),
        name="outproj",
    )(*xs, ma, mb, mc0, mc1, w_out, mod, ng, mod, mod, wr, rb)


def _route_kernel(lg_ref, info_ref, dest_ref, cnt_ref, carry_s, lower_s, *, n_exp, n_grp, blk):
    p = pl.program_id(0)
    i = pl.program_id(1)
    tm = lg_ref.shape[0]
    epg = n_exp // n_grp

    @pl.when((p == 0) & (i == 0))
    def _():
        carry_s[...] = jnp.zeros_like(carry_s)
        rr = lax.broadcasted_iota(jnp.int32, (tm, tm), 0)
        cc = lax.broadcasted_iota(jnp.int32, (tm, tm), 1)
        lower_s[...] = jnp.where(cc < rr, 1.0, 0.0).astype(BF)

    @pl.when((p == 1) & (i == 0))
    def _():
        cnt = carry_s[...]
        cnt_ref[...] = cnt
        nblk = jnp.floor((cnt + float(blk - 1)) * (1.0 / blk))
        hi = jnp.floor(nblk * (1.0 / 256.0))
        lo = nblk - 256.0 * hi
        r = lax.broadcasted_iota(jnp.int32, (LANES, LANES), 0)
        c = lax.broadcasted_iota(jnp.int32, (LANES, LANES), 1)
        upper = jnp.where(r < c, 1.0, 0.0).astype(BF)
        start_blk = 256.0 * _dot(hi.astype(BF), upper) + _dot(lo.astype(BF), upper)
        carry_s[...] = start_blk * float(blk)

    lg = lg_ref[...]
    lane = lax.broadcasted_iota(jnp.int32, (tm, LANES), 1).astype(F32)
    big = float(1 << 20)
    neg = -jnp.inf
    is_c = (lane >= n_exp) & (lane < n_exp + n_grp)
    lc = jnp.where(is_c, lg, neg)
    m = jnp.max(lc, axis=1, keepdims=True)
    gidx = jnp.min(jnp.where(lc == m, lane - n_exp, big), axis=1, keepdims=True)
    se = jnp.sum(jnp.where(is_c, jnp.exp(lc - m), 0.0), axis=1, keepdims=True)
    p_grp = 1.0 / se
    in_g = (lane >= gidx * epg) & (lane < (gidx + 1.0) * epg)
    lf = jnp.where(in_g, lg, neg)
    v1 = jnp.max(lf, axis=1, keepdims=True)
    i1 = jnp.min(jnp.where((lf == v1) & in_g, lane, big), axis=1, keepdims=True)
    in_g2 = in_g & (lane != i1)
    lf2 = jnp.where(in_g2, lg, neg)
    v2 = jnp.max(lf2, axis=1, keepdims=True)
    i2 = jnp.min(jnp.where((lf2 == v2) & in_g2, lane, big), axis=1, keepdims=True)
    e2 = jnp.exp(v2 - v1)
    den = 1.0 + e2
    w1 = p_grp * (1.0 / den)
    w2 = p_grp * (e2 / den)

    sel1 = lane == i1
    sel2 = lane == i2
    oh = jnp.where(sel1 | sel2, 1.0, 0.0)
    slot = carry_s[0:1, :] + _dot(lower_s[...], oh.astype(BF))
    d1 = jnp.sum(jnp.where(sel1, slot, 0.0), axis=1, keepdims=True)
    d2 = jnp.sum(jnp.where(sel2, slot, 0.0), axis=1, keepdims=True)
    carry_s[...] = carry_s[...] + jnp.sum(oh, axis=0, keepdims=True)

    @pl.when(p == 1)
    def _():
        info_ref[...] = jnp.where(lane == 0, w1, jnp.where(lane == 1, w2, 0.0))
        dd = jnp.where(lane == 0, d1, jnp.where(lane == 1, d2, 0.0))
        dt = jnp.transpose(dd)[0:SUBLANES, :].astype(jnp.int32)
        tt = dest_ref.shape[2]
        for j in range(dest_ref.shape[0]):
            dest_ref[j] = dt[:, j * tt:(j + 1) * tt]


def _route(lg, n_exp, n_grp, blk, tm):
    n = lg.shape[0]
    nt = n // tm
    per = math.gcd(nt, ROUTE_TILES)
    tr = per * tm
    assert blk & (blk - 1) == 0
    return pl.pallas_call(
        functools.partial(_route_kernel, n_exp=n_exp, n_grp=n_grp, blk=blk),
        grid=(2, n // tr),
        in_specs=[pl.BlockSpec((tr, LANES), lambda p, i: (i, 0))],
        out_specs=[
            pl.BlockSpec((tr, LANES), lambda p, i: (i * p, 0)),
            pl.BlockSpec((per, SUBLANES, tm), lambda p, i: (i * p, 0, 0)),
            pl.BlockSpec((SUBLANES, LANES), lambda p, i: (0, 0)),
        ],
        out_shape=[
            jax.ShapeDtypeStruct((n, LANES), F32),
            jax.ShapeDtypeStruct((nt, SUBLANES, tm), jnp.int32),
            jax.ShapeDtypeStruct((SUBLANES, LANES), F32),
        ],
        scratch_shapes=[pltpu.VMEM((SUBLANES, LANES), F32), pltpu.VMEM((tr, tr), BF)],
        compiler_params=_params("arbitrary", "arbitrary"),
        name="route",
    )(lg)


def _dispatch_kernel(cnt_ref, pst_ref, nu_ref, dest_hbm, h_ref, xs_hbm, dsm, isem, sem, zblk, *, blk, n_exp, ns):
    i = pl.program_id(0)
    per, _, tm = dsm.shape
    cp = pltpu.make_async_copy(dest_hbm.at[pl.ds(i * per, per)], dsm, isem)
    cp.start()
    cp.wait()

    def row_copy(j, t, k):
        return pltpu.make_async_copy(_slab_at(h_ref, j * tm + t, ns), _slab_at(xs_hbm, dsm[j, k, t], ns), sem)

    for j in range(per):
        def issue(tb, c):
            for u in range(DMA_UNROLL):
                for k in range(TOP_K):
                    row_copy(j, tb * DMA_UNROLL + u, k).start(priority=k % 2)
            return c

        lax.fori_loop(0, tm // DMA_UNROLL, issue, 0)

    for j in range(per):
        def drain(tb, c):
            for u in range(DMA_UNROLL):
                for k in range(TOP_K):
                    row_copy(j, tb * DMA_UNROLL + u, k).wait()
            return c

        lax.fori_loop(0, tm // DMA_UNROLL, drain, 0)

    @pl.when(i == pl.num_programs(0) - 1)
    def _():
        zblk[...] = jnp.zeros_like(zblk)

        def per_expert(e, c):
            cnt = cnt_ref[e]
            first = pst_ref[e] + cnt
            n_pad = (blk - cnt % blk) % blk

            def pad_copy(r):
                return pltpu.make_async_copy(_slab_at(zblk, 0, ns), _slab_at(xs_hbm, first + r, ns), sem)

            def pad_start(r, c2):
                pad_copy(r).start()
                return c2

            def pad_wait(r, c2):
                pad_copy(r).wait()
                return c2

            lax.fori_loop(0, n_pad, pad_start, 0)
            lax.fori_loop(0, n_pad, pad_wait, 0)
            return c

        lax.fori_loop(0, n_exp, per_expert, 0)

        def unused_block(b, c):
            z = pltpu.make_async_copy(zblk, _slab_at(xs_hbm, b, blk * ns), sem)
            z.start()
            z.wait()
            return c

        lax.fori_loop(nu_ref[0], xs_hbm.shape[0] // (blk * ns), unused_block, 0)


def _dispatch(cnt, pstart, n_used, dest, h, n_rows, ns, blk, tm):
    n = h.shape[0] // ns
    n_exp = cnt.shape[0]
    per = math.gcd(n // tm, ROUTE_TILES)
    return pl.pallas_call(
        functools.partial(_dispatch_kernel, blk=blk, n_exp=n_exp, ns=ns),
        grid_spec=pltpu.PrefetchScalarGridSpec(
            num_scalar_prefetch=3,
            grid=(n // (per * tm),),
            in_specs=[
                pl.BlockSpec(memory_space=pl.ANY),
                pl.BlockSpec((per * tm * ns, LANES), lambda i, c, p, u: (i, 0)),
            ],
            out_specs=pl.BlockSpec(memory_space=pl.ANY),
            scratch_shapes=[
                pltpu.SMEM((per, SUBLANES, tm), jnp.int32),
                pltpu.SemaphoreType.DMA,
                pltpu.SemaphoreType.DMA,
                pltpu.VMEM((blk * ns, LANES), h.dtype),
            ],
        ),
        out_shape=jax.ShapeDtypeStruct((n_rows * ns, LANES), h.dtype),
        compiler_params=_params("arbitrary"),
        name="dispatch",
    )(cnt, pstart, n_used, dest, h)


def _expert_kernel(be_ref, bi_ref, nu_ref, xs_ref, w1_ref, w3_ref, w2_ref, ys_ref, w1_s, w3_s, w2_s, *, blk):
    b = pl.program_id(0)
    used = b < nu_ref[0]

    @pl.when(jnp.logical_not(used))
    def _():
        ys_ref[...] = jnp.zeros_like(ys_ref)

    @pl.when(used & ((b == 0) | (be_ref[b] != be_ref[jnp.maximum(b - 1, 0)])))
    def _():
        w1_s[...] = w1_ref[...].astype(BF)
        w3_s[...] = w3_ref[...].astype(BF)
        w2_s[...] = w2_ref[...].astype(BF)

    @pl.when(used)
    def _():
        x = _unpack_rows(xs_ref, blk, BF)
        de = w1_s.shape[1]
        hms = []
        for cs in (slice(0, de // 2), slice(de // 2, de)):
            a = _dot(x, w1_s[:, cs])
            c = _dot(x, w3_s[:, cs])
            hms.append((a * jax.nn.sigmoid(a) * c).astype(BF))
        hm = jnp.concatenate(hms, axis=1)
        d = w2_s.shape[1]
        gw = 2 * LANES
        for g in range(d // 2 // gw):
            lo = _dot(hm, w2_s[:, g * gw:(g + 1) * gw])
            hi = _dot(hm, w2_s[:, d // 2 + g * gw:d // 2 + (g + 1) * gw])
            _pack_cols(lo, hi, ys_ref, g * (gw // LANES))


def _experts(blk_e, blk_i, n_used, xs, w1, w3, w2, layer, blk):
    d, de = w1.shape[2:]
    ns = _slab_rows(d)
    w_idx = lambda b, be, bi, nu: (layer, be[b], 0, 0)
    return pl.pallas_call(
        functools.partial(_expert_kernel, blk=blk),
        grid_spec=pltpu.PrefetchScalarGridSpec(
            num_scalar_prefetch=3,
            grid=(xs.shape[0] // (blk * ns),),
            in_specs=[
                pl.BlockSpec((blk * ns, LANES), lambda b, be, bi, nu: (bi[b], 0)),
                pl.BlockSpec((None, None, d, de), w_idx),
                pl.BlockSpec((None, None, d, de), w_idx),
                pl.BlockSpec((None, None, de, d), w_idx),
            ],
            out_specs=pl.BlockSpec((blk * ns, LANES), lambda b, be, bi, nu: (b, 0)),
            scratch_shapes=[pltpu.VMEM((d, de), BF), pltpu.VMEM((d, de), BF), pltpu.VMEM((de, d), BF)],
        ),
        out_shape=jax.ShapeDtypeStruct(xs.shape, xs.dtype),
        compiler_params=_params("arbitrary"),
        name="experts",
    )(blk_e, blk_i, n_used, xs, w1, w3, w2)


def _combine_kernel(*refs, geo, final):
    if final:
        dest_hbm, ys_hbm, x_ref, info_ref, g2_ref, fg_ref, o0_ref, o1_ref = refs[:8]
    else:
        dest_hbm, ys_hbm, x_ref, info_ref, g2_ref, o_ref = refs[:6]
    dsm, isem, sems, b00, b01, b10, b11 = refs[-7:]
    bufs = ((b00, b01), (b10, b11))
    i = pl.program_id(0)
    nt = pl.num_programs(0)
    tm, d = x_ref.shape
    ns = _slab_rows(d)

    def row_copy(slot, t, k):
        return pltpu.make_async_copy(_slab_at(ys_hbm, dsm[slot, k, t], ns), _slab_at(bufs[slot][k], t, ns),
                                     sems.at[slot])

    def start_gather(tile, slot):
        cp = pltpu.make_async_copy(dest_hbm.at[tile], dsm.at[slot], isem)
        cp.start()
        cp.wait()

        def issue(tb, c):
            for u in range(DMA_UNROLL):
                for k in range(TOP_K):
                    row_copy(slot, tb * DMA_UNROLL + u, k).start(priority=k % 2)
            return c

        lax.fori_loop(0, tm // DMA_UNROLL, issue, 0)

    def finish_gather(slot):
        def drain(tb, c):
            for u in range(DMA_UNROLL):
                for k in range(TOP_K):
                    row_copy(slot, tb * DMA_UNROLL + u, k).wait()
            return c

        lax.fori_loop(0, tm // DMA_UNROLL, drain, 0)

    @pl.when(i == 0)
    def _():
        start_gather(0, 0)

    def step(slot):
        @pl.when(i + 1 < nt)
        def _():
            start_gather(i + 1, 1 - slot)

        finish_gather(slot)
        y = (_unpack_rows(bufs[slot][0], tm, F32) * info_ref[:, 0:1]
             + _unpack_rows(bufs[slot][1], tm, F32) * info_ref[:, 1:2])
        x = x_ref[...] + g2_ref[...] * y
        if final:
            x = _rmsnorm(x, fg_ref[...])

            @pl.when(i < geo.n0t)
            def _():
                o0_ref[...] = x

            @pl.when(i >= geo.n0t)
            def _():
                o1_ref[...] = x
        else:
            o_ref[...] = x

    for slot in range(2):
        pl.when(i % 2 == slot)(functools.partial(step, slot))


def _combine(dest, ys, x, info, mod, geo, final_g=None):
    n, d = x.shape
    tm = geo.tm
    final = final_g is not None
    in_specs = [
        pl.BlockSpec(memory_space=pl.ANY),
        pl.BlockSpec(memory_space=pl.ANY),
        pl.BlockSpec((tm, d), lambda i: (i, 0)),
        pl.BlockSpec((tm, LANES), lambda i: (i, 0)),
        _mod_spec(geo, d, 5),
    ]
    args = [dest, ys, x, info, mod]
    if final:
        in_specs.append(pl.BlockSpec((1, d), lambda i: (0, 0)))
        args.append(final_g)
        out_specs = _two_group_specs(geo, d)
        out_shape = [jax.ShapeDtypeStruct((geo.n0, d), F32), jax.ShapeDtypeStruct((geo.n1, d), F32)]
    else:
        out_specs = pl.BlockSpec((tm, d), lambda i: (i, 0))
        out_shape = jax.ShapeDtypeStruct((n, d), F32)
    return pl.pallas_call(
        functools.partial(_combine_kernel, geo=geo, final=final),
        grid=(geo.nt,),
        in_specs=in_specs,
        out_specs=out_specs,
        out_shape=out_shape,
        scratch_shapes=[
            pltpu.SMEM((2, SUBLANES, tm), jnp.int32),
            pltpu.SemaphoreType.DMA,
            pltpu.SemaphoreType.DMA((2,)),
        ] + [pltpu.VMEM((tm * _slab_rows(d), LANES), ys.dtype)] * (2 * TOP_K),
        compiler_params=_params("arbitrary"),
        name="combine",
    )(*args)


def _block_tables(cnt_f, n_exp, blk, n_blocks):
    cnt = cnt_f[0, :n_exp].astype(jnp.int32)
    nblk = (cnt + (blk - 1)) // blk
    end_blk = jnp.cumsum(nblk)
    n_used = end_blk[-1]
    pstart = (end_blk - nblk) * blk
    b = jnp.minimum(jnp.arange(n_blocks, dtype=jnp.int32), n_used - 1)
    blk_e = jnp.minimum(jnp.sum(end_blk[None, :] <= b[:, None], axis=1), n_exp - 1).astype(jnp.int32)
    return cnt, pstart.astype(jnp.int32), blk_e, b, n_used.reshape(1).astype(jnp.int32)


def kernel(x_prompt, x_sample, c_prompt, c_sample, w_ada, b_ada, norm1_g, w_in, pool_w, pool_scale, conv_dw_w, conv_dw_b, conv_ln_g, conv_ln_b, conv_pw_w, conv_pw_b, fourier_w, w_out, norm2_g, router_coarse_w, router_coarse_b, router_fine_w, router_fine_b, expert_w1, expert_w3, expert_w2, final_g):
    b0, s0, d = x_prompt.shape
    b1, s1, _ = x_sample.shape
    depth = w_ada.shape[0]
    pw = pool_w.shape[1] * pool_w.shape[2]
    cw = conv_pw_w.shape[1]
    fw = fourier_w.shape[1]
    n_grp = router_coarse_w.shape[2]
    epg = router_fine_w.shape[3]
    n_exp = n_grp * epg
    assert n_exp + n_grp <= LANES and pool_w.shape[1] == len(POOL_WINDOWS) and d % (2 * LANES) == 0
    tm = math.gcd(math.gcd(s0, s1), 256)
    geo = _Geom(b0, s0, b1, s1, tm)
    geo_mm = _Geom(b0, s0, b1, s1, math.gcd(math.gcd(s0, s1), MM_ROWS))
    n = geo.n
    n_seq = b0 + b1
    blk = ROUTE_ROWS
    n_blocks = -(-(n * TOP_K + n_exp * (blk - 1)) // blk)

    c = jnp.concatenate([c_prompt, c_sample], axis=0)
    mod = _ada(c, w_ada, b_ada).reshape(depth, n_seq, 6, 1, d)

    xs = [x_prompt.reshape(geo.n0, d), x_sample.reshape(geo.n1, d)]
    for l in range(depth):
        ml = mod[l]
        za, v, zc = _inproj(xs, norm1_g[l].reshape(1, d), ml, w_in[l].astype(BF), geo_mm, pw, cw, fw)
        ma, mb = _mix(za, v, pool_w[l].astype(BF), pool_scale[l].reshape(1, pw), conv_dw_w[l],
                      conv_dw_b[l].reshape(1, cw), conv_ln_g[l].reshape(1, cw), conv_ln_b[l].reshape(1, cw),
                      conv_pw_w[l].astype(BF), conv_pw_b[l].reshape(1, cw), geo)
        fwl = fourier_w[l].astype(BF)
        mc0 = _fourier(zc, fwl, b0, s0, 0)
        mc1 = _fourier(zc, fwl, b1, s1, geo.n0)

        wr = jnp.concatenate([
            jnp.transpose(router_fine_w[l], (1, 0, 2)).reshape(d, n_exp),
            router_coarse_w[l],
            jnp.zeros((d, LANES - n_exp - n_grp), F32)], axis=1)
        rb = jnp.concatenate([router_fine_b[l].reshape(n_exp), router_coarse_b[l],
                              jnp.zeros((LANES - n_exp - n_grp,), F32)]).reshape(1, LANES)
        x, h, lg = _outproj(xs, ma, mb, mc0, mc1, w_out[l].astype(BF), ml, norm2_g[l].reshape(1, d),
                            wr.astype(BF), rb, geo_mm)

        info, dest, cnt_f = _route(lg, n_exp, n_grp, blk, tm)
        cnt, pstart, blk_e, blk_i, n_used = _block_tables(cnt_f, n_exp, blk, n_blocks)
        xsort = _dispatch(cnt, pstart, n_used, dest, h, n_blocks * blk, _slab_rows(d), blk, tm)
        ys = _experts(blk_e, blk_i, n_used, xsort, expert_w1, expert_w3, expert_w2, l, blk)
        if l + 1 < depth:
            xs = [_combine(dest, ys, x, info, ml, geo)]
        else:
            y0, y1 = _combine(dest, ys, x, info, ml, geo, final_g.reshape(1, d))
    return (y0.reshape(b0, s0, d), y1.reshape(b1, s1, d))
```
